```python
import jax, jax.numpy as jnp
from jax import lax
import numpy as np

D_MODEL = 4096
BATCH = 1
SEQ = 8192
DEPTH = 1
DEC_BATCH = 8
DEC_SEQ = 32
PAST_LEN = 4096

CHUNK = 64
HEAD_DIM = 128
N_HEADS_SB = D_MODEL // 256
N_HEADS_FOX = D_MODEL // 256
WIDTH_SB = N_HEADS_SB * HEAD_DIM
WIDTH_FOX = N_HEADS_FOX * HEAD_DIM
N_IN = 3 * WIDTH_SB + 3 * WIDTH_FOX + N_HEADS_FOX
IN_SPLITS = (WIDTH_SB, 2 * WIDTH_SB, 3 * WIDTH_SB, 3 * WIDTH_SB + WIDTH_FOX,
             3 * WIDTH_SB + 2 * WIDTH_FOX, 3 * WIDTH_SB + 3 * WIDTH_FOX)
N_EXPERTS = 32
TOP_K = 4
D_FF = D_MODEL
SWIGLU_ALPHA = 1.702
SWIGLU_LIMIT = 7.0
Q_BLOCK = 128
N_MOD = 6
EPS = 1e-6
SCALE = HEAD_DIM ** -0.5

kernel_name = 'stickbreak_fox_moe_streaming_step'


def rms_norm(x, g):
    x32 = x.astype(jnp.float32)
    return (x32 * lax.rsqrt(jnp.mean(x32 * x32, axis=-1, keepdims=True) + EPS) * g).astype(x.dtype)


def sb_attend(q, q_pos, k, v, k_pos):
    z = jnp.einsum('bqhd,bkhd->bhqk', q, k).astype(jnp.float32) * SCALE
    earlier = k_pos[None, :] < q_pos[:, None]
    log_stay = jnp.where(earlier, jax.nn.log_sigmoid(-z), 0.0)
    log_w = jax.nn.log_sigmoid(z) + lax.cumsum(log_stay, axis=3, reverse=True) - log_stay
    w = jnp.where(earlier, jnp.exp(log_w), 0.0)
    return jnp.einsum('bhqk,bkhd->bqhd', w.astype(v.dtype), v)


def fox_attend(q, f_q, q_pos, k, v, f_k, k_pos):
    z = jnp.einsum('bqhd,bkhd->bhqk', q, k).astype(jnp.float32) * SCALE
    decay = jnp.moveaxis(f_q, 1, 2)[..., :, None] - jnp.moveaxis(f_k, 1, 2)[..., None, :]
    causal = k_pos[None, :] <= q_pos[:, None]
    w = jax.nn.softmax(jnp.where(causal, z + decay, -jnp.inf), axis=-1)
    return jnp.einsum('bhqk,bkhd->bqhd', w.astype(v.dtype), v)


def sweep_query_blocks(attend, q_arrays, q_pos, kv_arrays):
    t = q_pos.shape[0]
    if t <= Q_BLOCK or t % Q_BLOCK:
        return attend(*q_arrays, q_pos, *kv_arrays)
    nb = t // Q_BLOCK

    def split(a):
        return jnp.moveaxis(a.reshape(a.shape[0], nb, Q_BLOCK, *a.shape[2:]), 1, 0)

    blocks = (tuple(split(a) for a in q_arrays), q_pos.reshape(nb, Q_BLOCK))
    out = lax.map(lambda blk: attend(*blk[0], blk[1], *kv_arrays), blocks)
    out = jnp.moveaxis(out, 0, 1)
    return out.reshape(out.shape[0], t, *out.shape[3:])


def hybrid_mixer(h, past_sb_k, past_sb_v, past_fox_k, past_fox_v, past_fox_logf,
                 w_in, b_forget, w_branch_sb, w_branch_fox, w_merge_gate, b_merge_gate, w_out):
    b, t, _ = h.shape
    p_len = past_sb_k.shape[1]
    proj = h @ w_in
    q_sb, k_sb, v_sb, q_fx, k_fx, v_fx, f_fx = jnp.split(proj, IN_SPLITS, axis=-1)
    q_sb, k_sb, v_sb = (a.reshape(b, t, N_HEADS_SB, HEAD_DIM) for a in (q_sb, k_sb, v_sb))
    q_fx, k_fx, v_fx = (a.reshape(b, t, N_HEADS_FOX, HEAD_DIM) for a in (q_fx, k_fx, v_fx))
    log_f = jax.nn.log_sigmoid((f_fx + b_forget).astype(jnp.float32))

    k_pos = jnp.arange(p_len + t)
    q_pos = p_len + jnp.arange(t)
    k_sb_all = jnp.concatenate([past_sb_k, k_sb], axis=1)
    v_sb_all = jnp.concatenate([past_sb_v, v_sb], axis=1)
    k_fx_all = jnp.concatenate([past_fox_k, k_fx], axis=1)
    v_fx_all = jnp.concatenate([past_fox_v, v_fx], axis=1)
    f_cum = jnp.cumsum(jnp.concatenate([past_fox_logf.astype(jnp.float32), log_f], axis=1), axis=1)

    o_sb = sweep_query_blocks(sb_attend, (q_sb,), q_pos, (k_sb_all, v_sb_all, k_pos))
    o_fx = sweep_query_blocks(fox_attend, (q_fx, f_cum[:, p_len:]), q_pos,
                              (k_fx_all, v_fx_all, f_cum, k_pos))

    branch_sb = o_sb.reshape(b, t, WIDTH_SB) @ w_branch_sb
    branch_fx = o_fx.reshape(b, t, WIDTH_FOX) @ w_branch_fox
    g_sb, g_fx = jnp.split(jax.nn.sigmoid(h @ w_merge_gate + b_merge_gate), 2, axis=-1)
    out = (g_sb * branch_sb + g_fx * branch_fx) @ w_out
    return out, (k_sb, v_sb, k_fx, v_fx, log_f)


def routed_ffn(h, w_router, b_router, w_exp_gate, b_exp_gate, w_exp_up, b_exp_up, w_exp_down, b_exp_down):
    b, t, d = h.shape
    tok = h.reshape(b * t, d)
    logits = (tok @ w_router + b_router).astype(jnp.float32)
    top_val, top_idx = lax.top_k(logits, TOP_K)
    top_w = jax.nn.softmax(top_val, axis=-1)
    combine = jnp.einsum('nk,nke->en', top_w, jax.nn.one_hot(top_idx, N_EXPERTS, dtype=jnp.float32))

    def add_expert(acc, xs):
        wg, bg, wu, bu, wd, bd, cw = xs
        gate = jnp.minimum(tok @ wg + bg, SWIGLU_LIMIT)
        up = jnp.clip(tok @ wu + bu, -SWIGLU_LIMIT, SWIGLU_LIMIT)
        act = gate * jax.nn.sigmoid(SWIGLU_ALPHA * gate) * (up + 1.0)
        return acc + cw[:, None].astype(tok.dtype) * (act @ wd + bd), None

    out, _ = lax.scan(add_expert, jnp.zeros_like(tok),
                      (w_exp_gate, b_exp_gate, w_exp_up, b_exp_up, w_exp_down, b_exp_down, combine))
    return out.reshape(b, t, d)


def encoder_layer(x, c, past_sb_k, past_sb_v, past_fox_k, past_fox_v, past_fox_logf,
                  ln_attn_g, ln_moe_g, w_ada, b_ada, w_in, b_forget, w_branch_sb, w_branch_fox,
                  w_merge_gate, b_merge_gate, w_out, w_router, b_router,
                  w_exp_gate, b_exp_gate, w_exp_up, b_exp_up, w_exp_down, b_exp_down):
    mod = jax.nn.silu(c) @ w_ada + b_ada
    sh_a, sc_a, g_a, sh_m, sc_m, g_m = jnp.split(mod[:, None, :], N_MOD, axis=-1)
    h = rms_norm(x, ln_attn_g) * (1.0 + sc_a) + sh_a
    mix, new_rows = hybrid_mixer(h, past_sb_k, past_sb_v, past_fox_k, past_fox_v, past_fox_logf,
                                 w_in, b_forget, w_branch_sb, w_branch_fox, w_merge_gate,
                                 b_merge_gate, w_out)
    x = x + g_a * mix
    h = rms_norm(x, ln_moe_g) * (1.0 + sc_m) + sh_m
    x = x + g_m * routed_ffn(h, w_router, b_router, w_exp_gate, b_exp_gate, w_exp_up, b_exp_up,
                             w_exp_down, b_exp_down)
    return x, new_rows


def setup_inputs(seed: int = 0) -> dict:
    key = jax.random.key(seed)
    ks = iter(jax.random.split(key, 40))

    def nrm(shape, scale):
        return scale * jax.random.normal(next(ks), shape, jnp.float32)

    L, D = DEPTH, D_MODEL
    return {
        'x_prompt': nrm((BATCH, SEQ, D), 1.0),
        'x_sample': nrm((DEC_BATCH, DEC_SEQ, D), 1.0),
        'cache_sb_k': nrm((L, DEC_BATCH, PAST_LEN, N_HEADS_SB, HEAD_DIM), 1.0),
        'cache_sb_v': nrm((L, DEC_BATCH, PAST_LEN, N_HEADS_SB, HEAD_DIM), 1.0),
        'cache_fox_k': nrm((L, DEC_BATCH, PAST_LEN, N_HEADS_FOX, HEAD_DIM), 1.0),
        'cache_fox_v': nrm((L, DEC_BATCH, PAST_LEN, N_HEADS_FOX, HEAD_DIM), 1.0),
        'cache_fox_logf': jax.nn.log_sigmoid(2.5 + nrm((L, DEC_BATCH, PAST_LEN, N_HEADS_FOX), 1.0)),
        'c_prompt': nrm((BATCH, D), 1.0),
        'c_sample': nrm((DEC_BATCH, D), 1.0),
        'ln_attn_g': 1.0 + nrm((L, D), 0.02),
        'ln_moe_g': 1.0 + nrm((L, D), 0.02),
        'w_ada': nrm((L, D, N_MOD * D), 0.5 * D ** -0.5),
        'b_ada': nrm((L, N_MOD * D), 0.02),
        'w_in': nrm((L, D, N_IN), D ** -0.5),
        'b_forget': jax.random.uniform(next(ks), (L, N_HEADS_FOX), jnp.float32, 1.0, 4.0),
        'w_branch_sb': nrm((L, WIDTH_SB, D), WIDTH_SB ** -0.5),
        'w_branch_fox': nrm((L, WIDTH_FOX, D), WIDTH_FOX ** -0.5),
        'w_merge_gate': nrm((L, D, 2 * D), D ** -0.5),
        'b_merge_gate': nrm((L, 2 * D), 0.02),
        'w_out': nrm((L, D, D), D ** -0.5),
        'w_router': nrm((L, D, N_EXPERTS), D ** -0.5),
        'b_router': nrm((L, N_EXPERTS), 0.01),
        'w_exp_gate': nrm((L, N_EXPERTS, D, D_FF), D ** -0.5),
        'b_exp_gate': nrm((L, N_EXPERTS, D_FF), 0.02),
        'w_exp_up': nrm((L, N_EXPERTS, D, D_FF), D ** -0.5),
        'b_exp_up': nrm((L, N_EXPERTS, D_FF), 0.02),
        'w_exp_down': nrm((L, N_EXPERTS, D_FF, D), D_FF ** -0.5),
        'b_exp_down': nrm((L, N_EXPERTS, D), 0.02),
        'ln_final_g': 1.0 + nrm((D,), 0.02),
    }


def reference(x_prompt, x_sample, cache_sb_k, cache_sb_v, cache_fox_k, cache_fox_v, cache_fox_logf,
              c_prompt, c_sample, ln_attn_g, ln_moe_g, w_ada, b_ada, w_in, b_forget,
              w_branch_sb, w_branch_fox, w_merge_gate, b_merge_gate, w_out, w_router, b_router,
              w_exp_gate, b_exp_gate, w_exp_up, b_exp_up, w_exp_down, b_exp_down, ln_final_g):
    hp, hs = x_prompt, x_sample
    bp = x_prompt.shape[0]
    rows_p, rows_s = [], []
    for layer in range(DEPTH):
        weights = (ln_attn_g[layer], ln_moe_g[layer], w_ada[layer], b_ada[layer], w_in[layer],
                   b_forget[layer], w_branch_sb[layer], w_branch_fox[layer], w_merge_gate[layer],
                   b_merge_gate[layer], w_out[layer], w_router[layer], b_router[layer],
                   w_exp_gate[layer], b_exp_gate[layer], w_exp_up[layer], b_exp_up[layer],
                   w_exp_down[layer], b_exp_down[layer])
        empty_sb = jnp.zeros((bp, 0, N_HEADS_SB, HEAD_DIM), hp.dtype)
        empty_fx = jnp.zeros((bp, 0, N_HEADS_FOX, HEAD_DIM), hp.dtype)
        empty_lf = jnp.zeros((bp, 0, N_HEADS_FOX), jnp.float32)
        hp, new_p = encoder_layer(hp, c_prompt, empty_sb, empty_sb, empty_fx, empty_fx, empty_lf, *weights)
        hs, new_s = encoder_layer(hs, c_sample, cache_sb_k[layer], cache_sb_v[layer], cache_fox_k[layer],
                                  cache_fox_v[layer], cache_fox_logf[layer], *weights)
        rows_p.append(new_p)
        rows_s.append(new_s)
    y_prompt = rms_norm(hp, ln_final_g)
    y_sample = rms_norm(hs, ln_final_g)
    p_sb_k = jnp.stack([r[0] for r in rows_p])
    p_sb_v = jnp.stack([r[1] for r in rows_p])
    p_fox_k = jnp.stack([r[2] for r in rows_p])
    p_fox_v = jnp.stack([r[3] for r in rows_p])
    p_fox_logf = jnp.stack([r[4] for r in rows_p])
    s_sb_k = jnp.stack([r[0] for r in rows_s])
    s_sb_v = jnp.stack([r[1] for r in rows_s])
    s_fox_k = jnp.stack([r[2] for r in rows_s])
    s_fox_v = jnp.stack([r[3] for r in rows_s])
    s_fox_logf = jnp.stack([r[4] for r in rows_s])
    return (y_prompt, y_sample, p_sb_k, p_sb_v, p_fox_k, p_fox_v, p_fox_logf,
            s_sb_k, s_sb_v, s_fox_k, s_fox_v, s_fox_logf)
```

```python
import functools

import jax
import jax.numpy as jnp
from jax import lax
from jax.experimental import pallas as pl
from jax.experimental.pallas import tpu as pltpu

F32 = jnp.float32
BF16 = jnp.bfloat16

HEAD_DIM = 128
TOP_K = 4
N_MOD = 6
EPS = 1e-6
SWIGLU_ALPHA = 1.702
SWIGLU_LIMIT = 7.0
NEG_BIG = -1e30

V7X_LANES = 128
V7X_VMEM_BUDGET_BYTES = 56 * 1024 * 1024


def _vmem_limit(block_bytes, scratch_bytes=0, temp_bytes=0):
    need = 2 * sum(block_bytes) + scratch_bytes + temp_bytes + (2 << 20)
    return int(min(max(need, 16 << 20), V7X_VMEM_BUDGET_BYTES))


def _params(semantics, vmem):
    return pltpu.CompilerParams(dimension_semantics=semantics, vmem_limit_bytes=vmem)


def _nbytes(shape, dtype):
    n = 1
    for s in shape:
        n *= s
    return n * jnp.dtype(dtype).itemsize


def _log_sigmoid(x):
    return jnp.minimum(x, 0.0) - jnp.log1p(jnp.exp(-jnp.abs(x)))


def _split3(x):
    hi = x.astype(BF16)
    r1 = x - hi.astype(F32)
    mid = r1.astype(BF16)
    lo = (r1 - mid.astype(F32)).astype(BF16)
    return hi, mid, lo


def _ada_kernel(c_ref, w_ref, b_ref, o_ref):
    c = c_ref[...]
    a = (c * jax.nn.sigmoid(c)).astype(BF16)
    o_ref[...] = jnp.dot(a, w_ref[...].astype(BF16), preferred_element_type=F32) + b_ref[...]


def _ada_mod(c_all, w_ada, b_ada):
    m, d = c_all.shape
    n = w_ada.shape[1]
    tn = 512
    return pl.pallas_call(
        _ada_kernel,
        grid=(n // tn,),
        in_specs=[pl.BlockSpec((m, d), lambda j: (0, 0)),
                  pl.BlockSpec((d, tn), lambda j: (0, j)),
                  pl.BlockSpec((1, tn), lambda j: (0, j))],
        out_specs=pl.BlockSpec((m, tn), lambda j: (0, j)),
        out_shape=jax.ShapeDtypeStruct((m, n), F32),
        compiler_params=_params(("arbitrary",), _vmem_limit(
            [_nbytes((d, tn), F32)], temp_bytes=_nbytes((d, tn), BF16))),
        name="ada_mod",
    )(c_all, w_ada, b_ada.reshape(1, n))


def _norm_mod(x, g, sc, sh):
    ms = jnp.mean(x * x, axis=-1, keepdims=True)
    h = x * lax.rsqrt(ms + EPS) * g
    return h * (1.0 + sc) + sh


def _norm1_kernel(x_ref, g_ref, sc_ref, sh_ref, wf_ref, bf_ref, h_ref, lf_ref):
    h = _norm_mod(x_ref[...], g_ref[...], sc_ref[...], sh_ref[...])
    hb = h.astype(BF16)
    h_ref[...] = hb
    f = jnp.dot(hb, wf_ref[...], preferred_element_type=F32)
    nh = lf_ref.shape[1]
    lf_ref[...] = _log_sigmoid(f[:, :nh] + bf_ref[...])


def _norm1(x, g, sc, sh, wf_pad, b_forget, tm):
    m, d = x.shape
    nh = b_forget.shape[1]
    rows = sc.shape[0]
    mod_spec = pl.BlockSpec((1, d), lambda i: (0, 0)) if rows == 1 else pl.BlockSpec((tm, d), lambda i: (i, 0))
    return pl.pallas_call(
        _norm1_kernel,
        grid=(m // tm,),
        in_specs=[pl.BlockSpec((tm, d), lambda i: (i, 0)),
                  pl.BlockSpec((1, d), lambda i: (0, 0)),
                  mod_spec, mod_spec,
                  pl.BlockSpec((d, V7X_LANES), lambda i: (0, 0)),
                  pl.BlockSpec((1, nh), lambda i: (0, 0))],
        out_specs=[pl.BlockSpec((tm, d), lambda i: (i, 0)),
                   pl.BlockSpec((tm, nh), lambda i: (i, 0))],
        out_shape=[jax.ShapeDtypeStruct((m, d), BF16), jax.ShapeDtypeStruct((m, nh), F32)],
        compiler_params=_params(("parallel",), _vmem_limit(
            [_nbytes((tm, d), F32) * (1 if rows == 1 else 3), _nbytes((tm, d), BF16)],
            temp_bytes=2 * _nbytes((tm, d), F32))),
        name="norm1",
    )(x, g, sc, sh, wf_pad, b_forget)


def _norm2_kernel(x_ref, g_ref, sc_ref, sh_ref, wrh_ref, wrl_ref, br_ref, h_ref, idx_ref, tw_ref):
    h = _norm_mod(x_ref[...], g_ref[...], sc_ref[...], sh_ref[...])
    h_ref[...] = h
    hb = h.astype(BF16)
    hl = (h - hb.astype(F32)).astype(BF16)
    wrh = wrh_ref[...]
    logits = (jnp.dot(hb, wrh, preferred_element_type=F32)
              + jnp.dot(hb, wrl_ref[...], preferred_element_type=F32)
              + jnp.dot(hl, wrh, preferred_element_type=F32)) + br_ref[...]
    lane = lax.broadcasted_iota(jnp.int32, logits.shape, 1)
    cur = logits
    vals, idxs = [], []
    for _ in range(TOP_K):
        mx = jnp.max(cur, axis=1, keepdims=True)
        ix = jnp.min(jnp.where(cur == mx, lane, V7X_LANES), axis=1, keepdims=True)
        vals.append(mx)
        idxs.append(ix)
        cur = jnp.where(lane == ix, -jnp.inf, cur)
    exps = [jnp.exp(v - vals[0]) for v in vals]
    den = exps[0]
    for e in exps[1:]:
        den = den + e
    idx_out = jnp.zeros(logits.shape, jnp.int32)
    w_out = jnp.zeros(logits.shape, F32)
    for k in range(TOP_K):
        idx_out = jnp.where(lane == k, idxs[k], idx_out)
        w_out = jnp.where(lane == k, exps[k] / den, w_out)
    idx_ref[...] = idx_out
    tw_ref[...] = w_out


def _norm2_router(x, g, sc, sh, wr_hi, wr_lo, br_pad, tm):
    m, d = x.shape
    rows = sc.shape[0]
    mod_spec = pl.BlockSpec((1, d), lambda i: (0, 0)) if rows == 1 else pl.BlockSpec((tm, d), lambda i: (i, 0))
    return pl.pallas_call(
        _norm2_kernel,
        grid=(m // tm,),
        in_specs=[pl.BlockSpec((tm, d), lambda i: (i, 0)),
                  pl.BlockSpec((1, d), lambda i: (0, 0)),
                  mod_spec, mod_spec,
                  pl.BlockSpec((d, V7X_LANES), lambda i: (0, 0)),
                  pl.BlockSpec((d, V7X_LANES), lambda i: (0, 0)),
                  pl.BlockSpec((1, V7X_LANES), lambda i: (0, 0))],
        out_specs=[pl.BlockSpec((tm, d), lambda i: (i, 0)),
                   pl.BlockSpec((tm, V7X_LANES), lambda i: (i, 0)),
                   pl.BlockSpec((tm, V7X_LANES), lambda i: (i, 0))],
        out_shape=[jax.ShapeDtypeStruct((m, d), F32),
                   jax.ShapeDtypeStruct((m, V7X_LANES), jnp.int32),
                   jax.ShapeDtypeStruct((m, V7X_LANES), F32)],
        compiler_params=_params(("parallel",), _vmem_limit(
            [_nbytes((tm, d), F32) * (2 if rows == 1 else 4)],
            temp_bytes=3 * _nbytes((tm, d), F32))),
        name="norm2_router",
    )(x, g, sc, sh, wr_hi, wr_lo, br_pad)


def _proj_kernel(a_ref, w_ref, *o_refs, scale, want_f32, want_bf16):
    acc = jnp.dot(a_ref[...], w_ref[...], preferred_element_type=F32)
    k = 0
    if want_f32:
        o_refs[k][...] = acc
        k += 1
    if want_bf16:
        o_refs[k][...] = (acc * scale).astype(BF16)


def _proj(a, w, col0, n, tm, tn, *, scale=1.0, want_f32, want_bf16):
    m, kd = a.shape
    cb0 = col0 // tn
    out_specs, out_shape, ob = [], [], []
    if want_f32:
        out_specs.append(pl.BlockSpec((tm, tn), lambda i, j: (i, j)))
        out_shape.append(jax.ShapeDtypeStruct((m, n), F32))
        ob.append(_nbytes((tm, tn), F32))
    if want_bf16:
        out_specs.append(pl.BlockSpec((tm, tn), lambda i, j: (i, j)))
        out_shape.append(jax.ShapeDtypeStruct((m, n), BF16))
        ob.append(_nbytes((tm, tn), BF16))
    return pl.pallas_call(
        functools.partial(_proj_kernel, scale=scale, want_f32=want_f32, want_bf16=want_bf16),
        grid=(m // tm, n // tn),
        in_specs=[pl.BlockSpec((tm, kd), lambda i, j: (i, 0)),
                  pl.BlockSpec((kd, tn), lambda i, j: (0, cb0 + j))],
        out_specs=out_specs,
        out_shape=out_shape,
        compiler_params=_params(("parallel", "arbitrary"), _vmem_limit(
            [_nbytes((tm, kd), BF16), _nbytes((kd, tn), BF16)] + ob,
            temp_bytes=_nbytes((tm, tn), F32))),
        name="proj",
    )(a, w)


def _merge_kernel(osb_ref, ofx_ref, h_ref, wsb_ref, wfx_ref, wg1_ref, wg2_ref, bg1_ref, bg2_ref, o_ref):
    h = h_ref[...]
    bs = jnp.dot(osb_ref[...], wsb_ref[...], preferred_element_type=F32)
    bx = jnp.dot(ofx_ref[...], wfx_ref[...], preferred_element_type=F32)
    g1 = jax.nn.sigmoid(jnp.dot(h, wg1_ref[...], preferred_element_type=F32) + bg1_ref[...])
    g2 = jax.nn.sigmoid(jnp.dot(h, wg2_ref[...], preferred_element_type=F32) + bg2_ref[...])
    o_ref[...] = (g1 * bs + g2 * bx).astype(BF16)


def _merge(o_sb, o_fx, h, w_bsb, w_bfx, w_gate, b_gate, tm, tn):
    m, d = h.shape
    ws = o_sb.shape[1]
    wx = o_fx.shape[1]
    nb = d // tn
    return pl.pallas_call(
        _merge_kernel,
        grid=(m // tm, nb),
        in_specs=[pl.BlockSpec((tm, ws), lambda i, j: (i, 0)),
                  pl.BlockSpec((tm, wx), lambda i, j: (i, 0)),
                  pl.BlockSpec((tm, d), lambda i, j: (i, 0)),
                  pl.BlockSpec((ws, tn), lambda i, j: (0, j)),
                  pl.BlockSpec((wx, tn), lambda i, j: (0, j)),
                  pl.BlockSpec((d, tn), lambda i, j: (0, j)),
                  pl.BlockSpec((d, tn), lambda i, j: (0, nb + j)),
                  pl.BlockSpec((1, tn), lambda i, j: (0, j)),
                  pl.BlockSpec((1, tn), lambda i, j: (0, nb + j))],
        out_specs=pl.BlockSpec((tm, tn), lambda i, j: (i, j)),
        out_shape=jax.ShapeDtypeStruct((m, d), BF16),
        compiler_params=_params(("parallel", "arbitrary"), _vmem_limit(
            [_nbytes((tm, ws + wx + d), BF16), _nbytes((ws + wx + 2 * d, tn), BF16), _nbytes((tm, tn), BF16)],
            temp_bytes=6 * _nbytes((tm, tn), F32))),
        name="merge",
    )(o_sb, o_fx, h, w_bsb, w_bfx, w_gate, w_gate, b_gate, b_gate)


def _outproj_kernel(a_ref, w_ref, x_ref, g_ref, o_ref):
    acc = jnp.dot(a_ref[...], w_ref[...], preferred_element_type=F32)
    o_ref[...] = x_ref[...] + g_ref[...] * acc


def _outproj(a, w, x, gate, tm, tn):
    m, kd = a.shape
    n = w.shape[1]
    rows = gate.shape[0]
    g_spec = pl.BlockSpec((1, tn), lambda i, j: (0, j)) if rows == 1 else pl.BlockSpec((tm, tn), lambda i, j: (i, j))
    return pl.pallas_call(
        _outproj_kernel,
        grid=(m // tm, n // tn),
        in_specs=[pl.BlockSpec((tm, kd), lambda i, j: (i, 0)),
                  pl.BlockSpec((kd, tn), lambda i, j: (0, j)),
                  pl.BlockSpec((tm, tn), lambda i, j: (i, j)),
                  g_spec],
        out_specs=pl.BlockSpec((tm, tn), lambda i, j: (i, j)),
        out_shape=jax.ShapeDtypeStruct((m, n), F32),
        compiler_params=_params(("parallel", "arbitrary"), _vmem_limit(
            [_nbytes((tm, kd), BF16), _nbytes((kd, tn), BF16), 3 * _nbytes((tm, tn), F32)],
            temp_bytes=_nbytes((tm, tn), F32))),
        name="outproj",
    )(a, w, x, gate)


def _cumsum_kernel(lf_ref, tri_ref, o_ref, *, chunk):
    t = lf_ref.shape[1]
    nh = lf_ref.shape[2]
    carry = jnp.zeros((1, nh), F32)
    start = 0
    while start < t:
        c = min(chunk, t - start)
        x = lf_ref[0, start:start + c, :]
        tri = tri_ref[:c, :c]
        hi, mid, lo = _split3(x)
        s = (jnp.dot(tri, hi, preferred_element_type=F32)
             + jnp.dot(tri, mid, preferred_element_type=F32)
             + jnp.dot(tri, lo, preferred_element_type=F32)) + carry
        o_ref[0, start:start + c, :] = s
        carry = s[c - 1:c, :]
        start += c


def _cumsum_time(lf, tri_incl, chunk):
    b, t, nh = lf.shape
    return pl.pallas_call(
        functools.partial(_cumsum_kernel, chunk=chunk),
        grid=(b,),
        in_specs=[pl.BlockSpec((1, t, nh), lambda i: (i, 0, 0)),
                  pl.BlockSpec((chunk, chunk), lambda i: (0, 0))],
        out_specs=pl.BlockSpec((1, t, nh), lambda i: (i, 0, 0)),
        out_shape=jax.ShapeDtypeStruct((b, t, nh), F32),
        compiler_params=_params(("parallel",), _vmem_limit(
            [2 * _nbytes((t, V7X_LANES), F32)], temp_bytes=4 << 20)),
        name="cumsum_logf",
    )(lf, tri_incl)


def _sb_tile(q, k, v, carry, acc, tri, mask):
    z = lax.dot_general(q, k, (((1,), (1,)), ((), ())), preferred_element_type=F32)
    ls = -(jnp.maximum(z, 0.0) + jnp.log1p(jnp.exp(-jnp.abs(z))))
    lsm = ls if mask is None else jnp.where(mask, ls, 0.0)
    hi = lsm.astype(BF16)
    lo = (lsm - hi.astype(F32)).astype(BF16)
    cs = jnp.dot(hi, tri, preferred_element_type=F32) + jnp.dot(lo, tri, preferred_element_type=F32)
    w = jnp.exp(z + ls + cs + carry)
    if mask is not None:
        w = jnp.where(mask, w, 0.0)
    acc = acc + jnp.dot(w.astype(BF16), v, preferred_element_type=F32)
    carry = carry + cs[:, :1] + lsm[:, :1]
    return carry, acc


def _strict_lower_mask(tq, tk):
    row = lax.broadcasted_iota(jnp.int32, (tq, tk), 0)
    col = lax.broadcasted_iota(jnp.int32, (tq, tk), 1)
    return col < row


def _sb_prompt_kernel(q_ref, k_ref, v_ref, tri_ref, o_ref, *, tq):
    i = pl.program_id(1)
    q = q_ref[...]
    tri = tri_ref[...]
    d0 = pl.multiple_of(i * tq, tq)
    carry = jnp.zeros((tq, 1), F32)
    acc = jnp.zeros((tq, HEAD_DIM), F32)
    carry, acc = _sb_tile(q, k_ref[pl.ds(d0, tq), :], v_ref[pl.ds(d0, tq), :], carry, acc, tri,
                          _strict_lower_mask(tq, tq))

    def body(n, c):
        r0 = pl.multiple_of((i - 1 - n) * tq, tq)
        return _sb_tile(q, k_ref[pl.ds(r0, tq), :], v_ref[pl.ds(r0, tq), :], c[0], c[1], tri, None)

    carry, acc = lax.fori_loop(0, i, body, (carry, acc))
    o_ref[...] = acc.astype(BF16)


def _sb_prompt(q, k, v, tri, n_heads, tq):
    t = q.shape[0]
    return pl.pallas_call(
        functools.partial(_sb_prompt_kernel, tq=tq),
        grid=(n_heads, t // tq),
        in_specs=[pl.BlockSpec((tq, HEAD_DIM), lambda h, i: (i, h)),
                  pl.BlockSpec((t, HEAD_DIM), lambda h, i: (0, h)),
                  pl.BlockSpec((t, HEAD_DIM), lambda h, i: (0, h)),
                  pl.BlockSpec((tq, tq), lambda h, i: (0, 0))],
        out_specs=pl.BlockSpec((tq, HEAD_DIM), lambda h, i: (i, h)),
        out_shape=jax.ShapeDtypeStruct((t, n_heads * HEAD_DIM), BF16),
        compiler_params=_params(("parallel", "arbitrary"), _vmem_limit(
            [2 * _nbytes((t, HEAD_DIM), BF16), _nbytes((tq, tq), BF16)],
            temp_bytes=10 * _nbytes((tq, tq), F32))),
        name="sb_prompt",
    )(q, k, v, tri)


def _sb_sample_kernel(q_ref, kn_ref, vn_ref, kp_ref, vp_ref, tri_ref, trin_ref, o_ref, *, tk):
    q = q_ref[0]
    tq = q.shape[0]
    tn = kn_ref.shape[1]
    p = kp_ref.shape[1]
    tri = tri_ref[...]
    carry = jnp.zeros((tq, 1), F32)
    acc = jnp.zeros((tq, HEAD_DIM), F32)
    carry, acc = _sb_tile(q, kn_ref[0], vn_ref[0], carry, acc, trin_ref[...], _strict_lower_mask(tq, tn))
    nchunks = p // tk

    def body(n, c):
        r0 = pl.multiple_of((nchunks - 1 - n) * tk, tk)
        kc = kp_ref[0, pl.ds(r0, tk), :].astype(BF16)
        vc = vp_ref[0, pl.ds(r0, tk), :].astype(BF16)
        return _sb_tile(q, kc, vc, c[0], c[1], tri, None)

    carry, acc = lax.fori_loop(0, nchunks, body, (carry, acc))
    o_ref[0] = acc.astype(BF16)


def _sb_sample(q, k_new, v_new, k_past, v_past, tri, tri_new, n_heads, tk):
    b, tq, _ = q.shape
    p = k_past.shape[1]
    new_spec = pl.BlockSpec((1, tq, HEAD_DIM), lambda bi, h: (bi, 0, h))
    past_spec = pl.BlockSpec((1, p, HEAD_DIM), lambda bi, h: (bi, 0, h))
    return pl.pallas_call(
        functools.partial(_sb_sample_kernel, tk=tk),
        grid=(b, n_heads),
        in_specs=[new_spec, new_spec, new_spec, past_spec, past_spec,
                  pl.BlockSpec((tk, tk), lambda bi, h: (0, 0)),
                  pl.BlockSpec((tq, tq), lambda bi, h: (0, 0))],
        out_specs=new_spec,
        out_shape=jax.ShapeDtypeStruct((b, tq, n_heads * HEAD_DIM), BF16),
        compiler_params=_params(("parallel", "arbitrary"), _vmem_limit(
            [2 * _nbytes((p, HEAD_DIM), F32)], temp_bytes=4 << 20)),
        name="sb_sample",
    )(q, k_new, v_new, k_past, v_past, tri, tri_new)


def _fox_tile(q, k, v, fk_row, fq_col, m, l, acc, mask):
    z = lax.dot_general(q, k, (((1,), (1,)), ((), ())), preferred_element_type=F32)
    s = z - fk_row
    if mask is not None:
        s = jnp.where(mask, s, NEG_BIG)
    m_new = jnp.maximum(m, jnp.max(s, axis=1, keepdims=True) + fq_col)
    pexp = jnp.exp(s + (fq_col - m_new))
    alpha = jnp.exp(m - m_new)
    l = alpha * l + jnp.sum(pexp, axis=1, keepdims=True)
    acc = alpha * acc + jnp.dot(pexp.astype(BF16), v, preferred_element_type=F32)
    return m_new, l, acc


def _lower_incl_mask(tq, tk):
    row = lax.broadcasted_iota(jnp.int32, (tq, tk), 0)
    col = lax.broadcasted_iota(jnp.int32, (tq, tk), 1)
    return col <= row


def _pick_head_column(f_cols, h):
    lane = lax.broadcasted_iota(jnp.int32, f_cols.shape, 1)
    return jnp.sum(jnp.where(lane == h, f_cols, 0.0), axis=1, keepdims=True)


def _fox_prompt_kernel(q_ref, k_ref, v_ref, fr_ref, fc_ref, o_ref, *, tq):
    h = pl.program_id(0)
    i = pl.program_id(1)
    q = q_ref[...]
    fq = _pick_head_column(fc_ref[...], h)
    d0 = pl.multiple_of(i * tq, tq)
    m = jnp.full((tq, 1), NEG_BIG, F32)
    l = jnp.zeros((tq, 1), F32)
    acc = jnp.zeros((tq, HEAD_DIM), F32)
    m, l, acc = _fox_tile(q, k_ref[pl.ds(d0, tq), :], v_ref[pl.ds(d0, tq), :], fr_ref[0, :, pl.ds(d0, tq)],
                          fq, m, l, acc, _lower_incl_mask(tq, tq))

    def body(n, c):
        r0 = pl.multiple_of(n * tq, tq)
        return _fox_tile(q, k_ref[pl.ds(r0, tq), :], v_ref[pl.ds(r0, tq), :], fr_ref[0, :, pl.ds(r0, tq)],
                         fq, c[0], c[1], c[2], None)

    m, l, acc = lax.fori_loop(0, i, body, (m, l, acc))
    o_ref[...] = (acc / l).astype(BF16)


def _fox_prompt(q, k, v, f_rows, f_cols, n_heads, tq):
    t = q.shape[0]
    return pl.pallas_call(
        functools.partial(_fox_prompt_kernel, tq=tq),
        grid=(n_heads, t // tq),
        in_specs=[pl.BlockSpec((tq, HEAD_DIM), lambda h, i: (i, h)),
                  pl.BlockSpec((t, HEAD_DIM), lambda h, i: (0, h)),
                  pl.BlockSpec((t, HEAD_DIM), lambda h, i: (0, h)),
                  pl.BlockSpec((1, 1, t), lambda h, i: (h, 0, 0)),
                  pl.BlockSpec((tq, n_heads), lambda h, i: (i, 0))],
        out_specs=pl.BlockSpec((tq, HEAD_DIM), lambda h, i: (i, h)),
        out_shape=jax.ShapeDtypeStruct((t, n_heads * HEAD_DIM), BF16),
        compiler_params=_params(("parallel", "arbitrary"), _vmem_limit(
            [2 * _nbytes((t, HEAD_DIM), BF16), _nbytes((8, t), F32)],
            temp_bytes=8 * _nbytes((tq, tq), F32))),
        name="fox_prompt",
    )(q, k, v, f_rows, f_cols)


def _fox_sample_kernel(q_ref, kn_ref, vn_ref, kp_ref, vp_ref, fr_ref, fc_ref, o_ref, *, tk):
    h = pl.program_id(1)
    q = q_ref[0]
    tq = q.shape[0]
    tn = kn_ref.shape[1]
    p = kp_ref.shape[1]
    fq = _pick_head_column(fc_ref[0], h)
    m = jnp.full((tq, 1), NEG_BIG, F32)
    l = jnp.zeros((tq, 1), F32)
    acc = jnp.zeros((tq, HEAD_DIM), F32)
    m, l, acc = _fox_tile(q, kn_ref[0], vn_ref[0], fr_ref[0, :, p:p + tn], fq, m, l, acc,
                          _lower_incl_mask(tq, tn))
    nchunks = p // tk

    def body(n, c):
        r0 = pl.multiple_of(n * tk, tk)
        kc = kp_ref[0, pl.ds(r0, tk), :].astype(BF16)
        vc = vp_ref[0, pl.ds(r0, tk), :].astype(BF16)
        return _fox_tile(q, kc, vc, fr_ref[0, :, pl.ds(r0, tk)], fq, c[0], c[1], c[2], None)

    m, l, acc = lax.fori_loop(0, nchunks, body, (m, l, acc))
    o_ref[0] = (acc / l).astype(BF16)


def _fox_sample(q, k_new, v_new, k_past, v_past, f_rows, f_cols_new, n_heads, tk):
    b, tq, _ = q.shape
    p = k_past.shape[1]
    new_spec = pl.BlockSpec((1, tq, HEAD_DIM), lambda bi, h: (bi, 0, h))
    past_spec = pl.BlockSpec((1, p, HEAD_DIM), lambda bi, h: (bi, 0, h))
    return pl.pallas_call(
        functools.partial(_fox_sample_kernel, tk=tk),
        grid=(b, n_heads),
        in_specs=[new_spec, new_spec, new_spec, past_spec, past_spec,
                  pl.BlockSpec((1, 1, p + tq), lambda bi, h: (bi * n_heads + h, 0, 0)),
                  pl.BlockSpec((1, tq, n_heads), lambda bi, h: (bi, 0, 0))],
        out_specs=new_spec,
        out_shape=jax.ShapeDtypeStruct((b, tq, n_heads * HEAD_DIM), BF16),
        compiler_params=_params(("parallel", "arbitrary"), _vmem_limit(
            [2 * _nbytes((p, HEAD_DIM), F32)], temp_bytes=4 << 20)),
        name="fox_sample",
    )(q, k_new, v_new, k_past, v_past, f_rows, f_cols_new)


def _row_copy(src_hbm, dst, row_src, row_dst, sem):
    return pltpu.make_async_copy(src_hbm.at[pl.ds(row_src, 1)], dst.at[pl.ds(row_dst, 1)], sem)


def _gather_kernel(cur_ref, nxt_ref, xp_hbm, xs_hbm, o_ref, buf, sem, *, n_p, tg):
    i = pl.program_id(0)
    n = pl.num_programs(0)

    def issue(idx_ref, slot):
        def body(r, _):
            s = idx_ref[r]

            @pl.when(s < n_p)
            def _():
                _row_copy(xp_hbm, buf.at[slot], s, r, sem.at[slot]).start()

            @pl.when(s >= n_p)
            def _():
                _row_copy(xs_hbm, buf.at[slot], s - n_p, r, sem.at[slot]).start()

            return 0

        lax.fori_loop(0, tg, body, 0)

    @pl.when(i == 0)
    def _():
        issue(cur_ref, 0)

    @pl.when(i + 1 < n)
    def _():
        issue(nxt_ref, (i + 1) % 2)

    slot = i % 2

    def wait_body(r, _):
        _row_copy(xp_hbm, buf.at[slot], 0, 0, sem.at[slot]).wait()
        return 0

    lax.fori_loop(0, tg, wait_body, 0)
    o_ref[...] = buf[slot].astype(BF16)


def _gather_tokens(src, x_p, x_s, n_rows, tg):
    d = x_p.shape[1]
    nt = n_rows // tg
    return pl.pallas_call(
        functools.partial(_gather_kernel, n_p=x_p.shape[0], tg=tg),
        grid=(nt,),
        in_specs=[pl.BlockSpec((tg,), lambda i: (i,), memory_space=pltpu.SMEM),
                  pl.BlockSpec((tg,), lambda i: (jnp.minimum(i + 1, nt - 1),), memory_space=pltpu.SMEM),
                  pl.BlockSpec(memory_space=pl.ANY), pl.BlockSpec(memory_space=pl.ANY)],
        out_specs=pl.BlockSpec((tg, d), lambda i: (i, 0)),
        scratch_shapes=[pltpu.VMEM((2, tg, d), F32), pltpu.SemaphoreType.DMA((2,))],
        out_shape=jax.ShapeDtypeStruct((n_rows, d), BF16),
        compiler_params=_params(("arbitrary",), _vmem_limit(
            [_nbytes((tg, d), BF16)], scratch_bytes=_nbytes((2, tg, d), F32), temp_bytes=_nbytes((tg, d), F32))),
        name="gather_tokens",
    )(src, src, x_p, x_s)


def _first_tile_of_expert(te_ref, t):
    prev = te_ref[jnp.maximum(t - 1, 0)]
    return jnp.logical_or(t == 0, te_ref[t] != prev)


def _moe_up_kernel(te_ref, tv_ref, x_ref, wg_ref, wu_ref, bg_ref, bu_ref, a_ref, wg_bf, wu_bf, *, sub):
    t = pl.program_id(1)

    @pl.when(_first_tile_of_expert(te_ref, t))
    def _():
        wg_bf[...] = wg_ref[0].astype(BF16)
        wu_bf[...] = wu_ref[0].astype(BF16)

    valid = tv_ref[t]
    tm = x_ref.shape[0]
    for s in range(tm // sub):
        rows = pl.ds(s * sub, sub)

        @pl.when(s * sub < valid)
        def _():
            x = x_ref[rows, :]
            gate = jnp.minimum(jnp.dot(x, wg_bf[...], preferred_element_type=F32) + bg_ref[0], SWIGLU_LIMIT)
            up = jnp.clip(jnp.dot(x, wu_bf[...], preferred_element_type=F32) + bu_ref[0],
                          -SWIGLU_LIMIT, SWIGLU_LIMIT)
            a_ref[rows, :] = (gate * jax.nn.sigmoid(SWIGLU_ALPHA * gate) * (up + 1.0)).astype(BF16)

        @pl.when(s * sub >= valid)
        def _():
            a_ref[rows, :] = jnp.zeros((sub, a_ref.shape[1]), BF16)


def _moe_up(tile_expert, tile_valid, x_sorted, w_gate, b_gate, w_up, b_up, tm, tn, sub):
    r, d = x_sorted.shape
    n_exp, _, f = w_gate.shape
    w_spec = pl.BlockSpec((1, d, tn), lambda n, t, te, tv: (te[t], 0, n))
    b_spec = pl.BlockSpec((1, 1, tn), lambda n, t, te, tv: (te[t], 0, n))
    return pl.pallas_call(
        functools.partial(_moe_up_kernel, sub=sub),
        grid_spec=pltpu.PrefetchScalarGridSpec(
            num_scalar_prefetch=2,
            grid=(f // tn, r // tm),
            in_specs=[pl.BlockSpec((tm, d), lambda n, t, te, tv: (t, 0)), w_spec, w_spec, b_spec, b_spec],
            out_specs=pl.BlockSpec((tm, tn), lambda n, t, te, tv: (t, n)),
            scratch_shapes=[pltpu.VMEM((d, tn), BF16), pltpu.VMEM((d, tn), BF16)]),
        out_shape=jax.ShapeDtypeStruct((r, f), BF16),
        compiler_params=_params(("arbitrary", "arbitrary"), _vmem_limit(
            [_nbytes((tm, d), BF16), 2 * _nbytes((d, tn), F32), _nbytes((tm, tn), BF16)],
            scratch_bytes=2 * _nbytes((d, tn), BF16), temp_bytes=6 * _nbytes((sub, tn), F32))),
        name="moe_up",
    )(tile_expert, tile_valid, x_sorted, w_gate, w_up,
      b_gate.reshape(n_exp, 1, f), b_up.reshape(n_exp, 1, f))


def _moe_down_kernel(te_ref, tv_ref, a_ref, wd_ref, bd_ref, cw_ref, y_ref, wd_bf, *, sub):
    t = pl.program_id(1)

    @pl.when(_first_tile_of_expert(te_ref, t))
    def _():
        wd_bf[...] = wd_ref[0].astype(BF16)

    valid = tv_ref[t]
    tm = a_ref.shape[0]
    for s in range(tm // sub):
        rows = pl.ds(s * sub, sub)

        @pl.when(s * sub < valid)
        def _():
            y = jnp.dot(a_ref[rows, :], wd_bf[...], preferred_element_type=F32) + bd_ref[0]
            y_ref[rows, :] = cw_ref[rows, :] * y

        @pl.when(s * sub >= valid)
        def _():
            y_ref[rows, :] = jnp.zeros((sub, y_ref.shape[1]), F32)


def _moe_down(tile_expert, tile_valid, act, w_down, b_down, cw_sorted, tm, tn, sub):
    r, f = act.shape
    n_exp, _, d = w_down.shape
    return pl.pallas_call(
        functools.partial(_moe_down_kernel, sub=sub),
        grid_spec=pltpu.PrefetchScalarGridSpec(
            num_scalar_prefetch=2,
            grid=(d // tn, r // tm),
            in_specs=[pl.BlockSpec((tm, f), lambda n, t, te, tv: (t, 0)),
                      pl.BlockSpec((1, f, tn), lambda n, t, te, tv: (te[t], 0, n)),
                      pl.BlockSpec((1, 1, tn), lambda n, t, te, tv: (te[t], 0, n)),
                      pl.BlockSpec((tm, 1), lambda n, t, te, tv: (t, 0))],
            out_specs=pl.BlockSpec((tm, tn), lambda n, t, te, tv: (t, n)),
            scratch_shapes=[pltpu.VMEM((f, tn), BF16)]),
        out_shape=jax.ShapeDtypeStruct((r, d), F32),
        compiler_params=_params(("arbitrary", "arbitrary"), _vmem_limit(
            [_nbytes((tm, f), BF16), _nbytes((f, tn), F32), _nbytes((tm, tn), F32),
             _nbytes((tm, V7X_LANES), F32)],
            scratch_bytes=_nbytes((f, tn), BF16), temp_bytes=3 * _nbytes((sub, tn), F32))),
        name="moe_down",
    )(tile_expert, tile_valid, act, w_down, b_down.reshape(n_exp, 1, d), cw_sorted)


def _combine_kernel(cur_ref, nxt_ref, y_hbm, xp_ref, xs_ref, gp_ref, gs_ref, lnf_ref, op_ref, os_ref, buf, sem,
                    *, n_tiles_p, tc):
    i = pl.program_id(0)
    n = pl.num_programs(0)

    def issue(idx_ref, slot):
        def body(r, _):
            for k in range(TOP_K):
                _row_copy(y_hbm, buf.at[slot, k], idx_ref[r * TOP_K + k], r, sem.at[slot]).start()
            return 0

        lax.fori_loop(0, tc, body, 0)

    @pl.when(i == 0)
    def _():
        issue(cur_ref, 0)

    @pl.when(i + 1 < n)
    def _():
        issue(nxt_ref, (i + 1) % 2)

    slot = i % 2

    def wait_body(r, _):
        _row_copy(y_hbm, buf.at[slot, 0], 0, 0, sem.at[slot]).wait()
        return 0

    lax.fori_loop(0, tc * TOP_K, wait_body, 0)
    ffn = (buf[slot, 0] + buf[slot, 1]) + (buf[slot, 2] + buf[slot, 3])

    def finish(x, g):
        x2 = x + g * ffn
        ms = jnp.mean(x2 * x2, axis=-1, keepdims=True)
        return x2 * lax.rsqrt(ms + EPS) * lnf_ref[...]

    @pl.when(i < n_tiles_p)
    def _():
        op_ref[...] = finish(xp_ref[...], gp_ref[...])

    @pl.when(i >= n_tiles_p)
    def _():
        os_ref[...] = finish(xs_ref[...], gs_ref[...])


def _combine(pos_flat, y_sorted, x1_p, x1_s, gm_p, gm_s, ln_final_g, tc):
    n_p, d = x1_p.shape
    n_s = x1_s.shape[0]
    ntp = n_p // tc
    nts = n_s // tc
    nt = ntp + nts
    p_map = lambda i: (jnp.minimum(i, ntp - 1), 0)
    s_map = lambda i: (jnp.maximum(i - ntp, 0), 0)
    tp = tc * TOP_K
    return pl.pallas_call(
        functools.partial(_combine_kernel, n_tiles_p=ntp, tc=tc),
        grid=(nt,),
        in_specs=[pl.BlockSpec((tp,), lambda i: (i,), memory_space=pltpu.SMEM),
                  pl.BlockSpec((tp,), lambda i: (jnp.minimum(i + 1, nt - 1),), memory_space=pltpu.SMEM),
                  pl.BlockSpec(memory_space=pl.ANY),
                  pl.BlockSpec((tc, d), p_map),
                  pl.BlockSpec((tc, d), s_map),
                  pl.BlockSpec((1, d), lambda i: (0, 0)),
                  pl.BlockSpec((tc, d), s_map),
                  pl.BlockSpec((1, d), lambda i: (0, 0))],
        out_specs=[pl.BlockSpec((tc, d), p_map), pl.BlockSpec((tc, d), s_map)],
        scratch_shapes=[pltpu.VMEM((2, TOP_K, tc, d), F32), pltpu.SemaphoreType.DMA((2,))],
        out_shape=[jax.ShapeDtypeStruct((n_p, d), F32), jax.ShapeDtypeStruct((n_s, d), F32)],
        compiler_params=_params(("arbitrary",), _vmem_limit(
            [5 * _nbytes((tc, d), F32)], scratch_bytes=_nbytes((2, TOP_K, tc, d), F32),
            temp_bytes=3 * _nbytes((tc, d), F32))),
        name="combine_final",
    )(pos_flat, pos_flat, y_sorted, x1_p, x1_s, gm_p, gm_s, ln_final_g)


def _route_metadata(top_idx, n_exp, tm):
    n_tok, k = top_idx.shape
    n_pairs = n_tok * k
    e_flat = top_idx.reshape(n_pairs)
    onehot = (e_flat[:, None] == jnp.arange(n_exp, dtype=jnp.int32)[None, :]).astype(jnp.int32)
    counts = jnp.sum(onehot, axis=0)
    rank = jnp.sum((jnp.cumsum(onehot, axis=0) - 1) * onehot, axis=1)
    padded = ((counts + tm - 1) // tm) * tm
    ends = jnp.cumsum(padded)
    starts = ends - padded
    pos = (starts[e_flat] + rank).astype(jnp.int32)
    n_tiles = n_pairs // tm + n_exp
    n_rows = n_tiles * tm
    tok = (jnp.arange(n_pairs, dtype=jnp.int32) // k)
    src = jnp.zeros((n_rows,), jnp.int32).at[pos].set(tok)
    tile_start = jnp.arange(n_tiles, dtype=jnp.int32) * tm
    tile_expert = jnp.minimum(jnp.searchsorted(ends, tile_start, side="right"), n_exp - 1).astype(jnp.int32)
    tile_valid = jnp.clip(counts[tile_expert] - (tile_start - starts[tile_expert]), 0, tm).astype(jnp.int32)
    return pos, src, tile_expert, tile_valid, n_rows


def _tri_strict(n):
    j = jnp.arange(n)[:, None]
    s = jnp.arange(n)[None, :]
    return (j > s).astype(BF16)


def _tri_incl(n):
    t = jnp.arange(n)[:, None]
    s = jnp.arange(n)[None, :]
    return (s <= t).astype(BF16)


def kernel(x_prompt, x_sample, cache_sb_k, cache_sb_v, cache_fox_k, cache_fox_v, cache_fox_logf, c_prompt, c_sample, ln_attn_g, ln_moe_g, w_ada, b_ada, w_in, b_forget, w_branch_sb, w_branch_fox, w_merge_gate, b_merge_gate, w_out, w_router, b_router, w_exp_gate, b_exp_gate, w_exp_up, b_exp_up, w_exp_down, b_exp_down, ln_final_g):
    bp, seq, d = x_prompt.shape
    bs, dseq, _ = x_sample.shape
    depth = w_in.shape[0]
    assert depth == 1 and bp == 1
    n_heads = b_forget.shape[1]
    width = n_heads * HEAD_DIM
    n_exp = w_router.shape[2]
    past = cache_sb_k.shape[2]
    n_p = bp * seq
    n_s = bs * dseq
    scale = HEAD_DIM ** -0.5

    c_all = jnp.concatenate([c_prompt, c_sample], axis=0)
    pad = (-c_all.shape[0]) % 8
    c_all = jnp.pad(c_all, ((0, pad), (0, 0)))
    mod = _ada_mod(c_all, w_ada[0], b_ada[0])
    mod_p = jnp.split(mod[0:1], N_MOD, axis=-1)
    mod_s = [jnp.repeat(m_, dseq, axis=0) for m_ in jnp.split(mod[1:1 + bs], N_MOD, axis=-1)]

    w_in_b = w_in[0].astype(BF16)
    wf_pad = jnp.pad(w_in_b[:, 6 * width:6 * width + n_heads], ((0, 0), (0, V7X_LANES - n_heads)))
    w_bsb = w_branch_sb[0].astype(BF16)
    w_bfx = w_branch_fox[0].astype(BF16)
    w_gate = w_merge_gate[0].astype(BF16)
    b_gate = b_merge_gate[0].reshape(1, 2 * d)
    w_o = w_out[0].astype(BF16)
    wr = jnp.pad(w_router[0], ((0, 0), (0, V7X_LANES - n_exp)))
    wr_hi = wr.astype(BF16)
    wr_lo = (wr - wr_hi.astype(F32)).astype(BF16)
    br_pad = jnp.pad(b_router[0].reshape(1, n_exp), ((0, 0), (0, V7X_LANES - n_exp)), constant_values=-jnp.inf)
    g_attn = ln_attn_g[0].reshape(1, d)
    g_moe = ln_moe_g[0].reshape(1, d)
    bf = b_forget[0].reshape(1, n_heads)

    tq = 256
    tri_q = _tri_strict(tq)
    tri_n = _tri_strict(dseq)
    tri_c = _tri_incl(256)

    def stream(x2d, mods, tm_norm, tm_mm, tn_mm):
        sh_a, sc_a, g_a = mods[0], mods[1], mods[2]
        h, lf = _norm1(x2d, g_attn, sc_a, sh_a, wf_pad, bf, tm_norm)
        pj = functools.partial(_proj, h, w_in_b, tm=tm_mm, tn=tn_mm)
        q_sb = pj(0 * width, width, scale=scale, want_f32=False, want_bf16=True)[0]
        k_sb, k_sb_b = pj(1 * width, width, want_f32=True, want_bf16=True)
        v_sb, v_sb_b = pj(2 * width, width, want_f32=True, want_bf16=True)
        q_fx = pj(3 * width, width, scale=scale, want_f32=False, want_bf16=True)[0]
        k_fx, k_fx_b = pj(4 * width, width, want_f32=True, want_bf16=True)
        v_fx, v_fx_b = pj(5 * width, width, want_f32=True, want_bf16=True)
        return h, lf, g_a, (q_sb, k_sb, v_sb, k_sb_b, v_sb_b), (q_fx, k_fx, v_fx, k_fx_b, v_fx_b)

    xp2 = x_prompt.reshape(n_p, d)
    h_p, lf_p, ga_p, sb_p, fx_p = stream(xp2, mod_p, 256, 1024, 512)
    o_sb_p = _sb_prompt(sb_p[0], sb_p[3], sb_p[4], tri_q, n_heads, tq)
    fcum_p = _cumsum_time(lf_p.reshape(1, n_p, n_heads), tri_c, 256)[0]
    f_rows_p = fcum_p.T.reshape(n_heads, 1, n_p)
    o_fx_p = _fox_prompt(fx_p[0], fx_p[3], fx_p[4], f_rows_p, fcum_p, n_heads, tq)
    merged_p = _merge(o_sb_p, o_fx_p, h_p, w_bsb, w_bfx, w_gate, b_gate, 1024, 256)
    x1_p = _outproj(merged_p, w_o, xp2, ga_p, 1024, 512)

    xs2 = x_sample.reshape(n_s, d)
    h_s, lf_s, ga_s, sb_s, fx_s = stream(xs2, mod_s, n_s, n_s, 512)
    r3 = lambda a: a.reshape(bs, dseq, width)
    c3 = lambda a: a[0].reshape(bs, past, width)
    o_sb_s = _sb_sample(r3(sb_s[0]), r3(sb_s[3]), r3(sb_s[4]), c3(cache_sb_k), c3(cache_sb_v),
                        tri_q, tri_n, n_heads, tq)
    lf_all = jnp.concatenate([cache_fox_logf[0].astype(F32), lf_s.reshape(bs, dseq, n_heads)], axis=1)
    fcum_s = _cumsum_time(lf_all, tri_c, 256)
    f_rows_s = jnp.swapaxes(fcum_s, 1, 2).reshape(bs * n_heads, 1, past + dseq)
    o_fx_s = _fox_sample(r3(fx_s[0]), r3(fx_s[3]), r3(fx_s[4]), c3(cache_fox_k), c3(cache_fox_v),
                         f_rows_s, fcum_s[:, past:, :], n_heads, tq)
    merged_s = _merge(o_sb_s.reshape(n_s, width), o_fx_s.reshape(n_s, width), h_s,
                      w_bsb, w_bfx, w_gate, b_gate, n_s, 256)
    x1_s = _outproj(merged_s, w_o, xs2, ga_s, n_s, 512)

    h2_p, idx_p, tw_p = _norm2_router(x1_p, g_moe, mod_p[4], mod_p[3], wr_hi, wr_lo, br_pad, 256)
    h2_s, idx_s, tw_s = _norm2_router(x1_s, g_moe, mod_s[4], mod_s[3], wr_hi, wr_lo, br_pad, n_s)
    top_idx = jnp.concatenate([idx_p[:, :TOP_K], idx_s[:, :TOP_K]], axis=0)
    top_w = jnp.concatenate([tw_p[:, :TOP_K], tw_s[:, :TOP_K]], axis=0)
    tm_moe, sub = 512, 128
    pos, src, tile_expert, tile_valid, n_rows = _route_metadata(top_idx, n_exp, tm_moe)
    cw_sorted = jnp.zeros((n_rows, 1), F32).at[pos, 0].set(top_w.reshape(-1))
    x_sorted = _gather_tokens(src, h2_p, h2_s, n_rows, 256)
    act = _moe_up(tile_expert, tile_valid, x_sorted, w_exp_gate[0], b_exp_gate[0], w_exp_up[0], b_exp_up[0],
                  tm_moe, 512, sub)
    y_sorted = _moe_down(tile_expert, tile_valid, act, w_exp_down[0], b_exp_down[0], cw_sorted, tm_moe, 1024, sub)
    y_p, y_s = _combine(pos, y_sorted, x1_p, x1_s, mod_p[5], mod_s[5], ln_final_g.reshape(1, d), 128)

    def rows5(a, b_, t_):
        return a.reshape(depth, b_, t_, n_heads, HEAD_DIM)

    return (y_p.reshape(bp, seq, d), y_s.reshape(bs, dseq, d),
            rows5(sb_p[1], bp, seq), rows5(sb_p[2], bp, seq), rows5(fx_p[1], bp, seq), rows5(fx_p[2], bp, seq),
            lf_p.reshape(depth, bp, seq, n_heads),
            rows5(sb_s[1], bs, dseq), rows5(sb_s[2], bs, dseq), rows5(fx_s[1], bs, dseq), rows5(fx_s[2], bs, dseq),
            lf_s.reshape(depth, bs, dseq, n_heads))
```

```python
import functools

import jax
import jax.numpy as jnp
from jax import lax
from jax.experimental import pallas as pl
from jax.experimental.pallas import tpu as pltpu

F32 = jnp.float32
BF16 = jnp.bfloat16

HEAD_DIM = 128
TOP_K = 4
N_MOD = 6
EPS = 1e-6
SWIGLU_ALPHA = 1.702
SWIGLU_LIMIT = 7.0
NEG_BIG = -1e30
LOG2E = 1.4426950408889634

V7X_LANES = 128
V7X_MXU_DIM = 256
V7X_VMEM_BUDGET_BYTES = 56 * 1024 * 1024


def _vmem_limit(block_bytes, scratch_bytes=0, temp_bytes=0):
    need = 2 * sum(block_bytes) + scratch_bytes + temp_bytes + (2 << 20)
    return int(min(max(need, 16 << 20), V7X_VMEM_BUDGET_BYTES))


def _params(semantics, vmem):
    return pltpu.CompilerParams(dimension_semantics=semantics, vmem_limit_bytes=vmem)


def _nbytes(shape, dtype):
    n = 1
    for s in shape:
        n *= s
    return n * jnp.dtype(dtype).itemsize


def _log_sigmoid(x):
    return jnp.minimum(x, 0.0) - jnp.log1p(jnp.exp(-jnp.abs(x)))


def _split3(x):
    hi = x.astype(BF16)
    r1 = x - hi.astype(F32)
    mid = r1.astype(BF16)
    lo = (r1 - mid.astype(F32)).astype(BF16)
    return hi, mid, lo


def _ada_kernel(c_ref, w_ref, b_ref, o_ref):
    c = c_ref[...]
    a = (c * jax.nn.sigmoid(c)).astype(BF16)
    o_ref[...] = jnp.dot(a, w_ref[...].astype(BF16), preferred_element_type=F32) + b_ref[...]


def _ada_mod(c_all, w_ada, b_ada):
    m, d = c_all.shape
    n = w_ada.shape[1]
    tn = 512
    return pl.pallas_call(
        _ada_kernel,
        grid=(n // tn,),
        in_specs=[pl.BlockSpec((m, d), lambda j: (0, 0)),
                  pl.BlockSpec((d, tn), lambda j: (0, j)),
                  pl.BlockSpec((1, tn), lambda j: (0, j))],
        out_specs=pl.BlockSpec((m, tn), lambda j: (0, j)),
        out_shape=jax.ShapeDtypeStruct((m, n), F32),
        compiler_params=_params(("arbitrary",), _vmem_limit(
            [_nbytes((d, tn), F32)], temp_bytes=_nbytes((d, tn), BF16))),
        name="ada_mod",
    )(c_all, w_ada, b_ada.reshape(1, n))


def _norm_mod(x, g, sc, sh):
    ms = jnp.mean(x * x, axis=-1, keepdims=True)
    h = x * lax.rsqrt(ms + EPS) * g
    return h * (1.0 + sc) + sh


def _norm1_kernel(x_ref, g_ref, sc_ref, sh_ref, wf_ref, bf_ref, h_ref, lf_ref):
    h = _norm_mod(x_ref[...], g_ref[...], sc_ref[...], sh_ref[...])
    hb = h.astype(BF16)
    h_ref[...] = hb
    f = jnp.dot(hb, wf_ref[...], preferred_element_type=F32)
    nh = lf_ref.shape[1]
    lf_ref[...] = _log_sigmoid(f[:, :nh] + bf_ref[...])


def _norm1(x, g, sc, sh, wf_pad, b_forget, tm):
    m, d = x.shape
    nh = b_forget.shape[1]
    rows = sc.shape[0]
    mod_spec = pl.BlockSpec((1, d), lambda i: (0, 0)) if rows == 1 else pl.BlockSpec((tm, d), lambda i: (i, 0))
    return pl.pallas_call(
        _norm1_kernel,
        grid=(m // tm,),
        in_specs=[pl.BlockSpec((tm, d), lambda i: (i, 0)),
                  pl.BlockSpec((1, d), lambda i: (0, 0)),
                  mod_spec, mod_spec,
                  pl.BlockSpec((d, V7X_LANES), lambda i: (0, 0)),
                  pl.BlockSpec((1, nh), lambda i: (0, 0))],
        out_specs=[pl.BlockSpec((tm, d), lambda i: (i, 0)),
                   pl.BlockSpec((tm, nh), lambda i: (i, 0))],
        out_shape=[jax.ShapeDtypeStruct((m, d), BF16), jax.ShapeDtypeStruct((m, nh), F32)],
        compiler_params=_params(("parallel",), _vmem_limit(
            [_nbytes((tm, d), F32) * (1 if rows == 1 else 3), _nbytes((tm, d), BF16)],
            temp_bytes=2 * _nbytes((tm, d), F32))),
        name="norm1",
    )(x, g, sc, sh, wf_pad, b_forget)


def _norm2_kernel(x_ref, g_ref, sc_ref, sh_ref, wrh_ref, wrl_ref, br_ref, *rest, n_real):
    h_ref, idx_ref, tw_ref = rest[-3:]
    i = pl.program_id(0)

    @pl.when(i < n_real)
    def _():
        _norm2_tile(x_ref, g_ref, sc_ref, sh_ref, wrh_ref, wrl_ref, br_ref, h_ref, idx_ref, tw_ref)

    @pl.when(i >= n_real)
    def _():
        h_ref[...] = jnp.zeros(h_ref.shape, F32)
        idx_ref[...] = jnp.zeros(idx_ref.shape, jnp.int32)
        tw_ref[...] = jnp.zeros(tw_ref.shape, F32)


def _norm2_tile(x_ref, g_ref, sc_ref, sh_ref, wrh_ref, wrl_ref, br_ref, h_ref, idx_ref, tw_ref):
    h = _norm_mod(x_ref[...], g_ref[...], sc_ref[...], sh_ref[...])
    h_ref[...] = h
    hb = h.astype(BF16)
    hl = (h - hb.astype(F32)).astype(BF16)
    wrh = wrh_ref[...]
    logits = (jnp.dot(hb, wrh, preferred_element_type=F32)
              + jnp.dot(hb, wrl_ref[...], preferred_element_type=F32)
              + jnp.dot(hl, wrh, preferred_element_type=F32)) + br_ref[...]
    lane = lax.broadcasted_iota(jnp.int32, logits.shape, 1)
    cur = logits
    vals, idxs = [], []
    for _ in range(TOP_K):
        mx = jnp.max(cur, axis=1, keepdims=True)
        ix = jnp.min(jnp.where(cur == mx, lane, V7X_LANES), axis=1, keepdims=True)
        vals.append(mx)
        idxs.append(ix)
        cur = jnp.where(lane == ix, -jnp.inf, cur)
    exps = [jnp.exp(v - vals[0]) for v in vals]
    den = exps[0]
    for e in exps[1:]:
        den = den + e
    idx_out = jnp.zeros(logits.shape, jnp.int32)
    w_out = jnp.zeros(logits.shape, F32)
    for k in range(TOP_K):
        idx_out = jnp.where(lane == k, idxs[k], idx_out)
        w_out = jnp.where(lane == k, exps[k] / den, w_out)
    idx_ref[...] = idx_out
    tw_ref[...] = w_out


def _norm2_router(x, g, sc, sh, wr_hi, wr_lo, br_pad, tm, n_total, row_off, prev=None):
    m, d = x.shape
    rows = sc.shape[0]
    assert row_off % tm == 0 and (n_total - row_off - m) % tm == 0
    off = row_off // tm
    n_real = m // tm
    n_zero = (n_total - row_off - m) // tm if prev is None else 0
    real = lambda i: jnp.minimum(i, n_real - 1)
    mod_spec = pl.BlockSpec((1, d), lambda i: (0, 0)) if rows == 1 else pl.BlockSpec((tm, d), lambda i: (real(i), 0))
    in_specs = [pl.BlockSpec((tm, d), lambda i: (real(i), 0)),
                pl.BlockSpec((1, d), lambda i: (0, 0)),
                mod_spec, mod_spec,
                pl.BlockSpec((d, V7X_LANES), lambda i: (0, 0)),
                pl.BlockSpec((d, V7X_LANES), lambda i: (0, 0)),
                pl.BlockSpec((1, V7X_LANES), lambda i: (0, 0))]
    args = [x, g, sc, sh, wr_hi, wr_lo, br_pad]
    aliases = {}
    if prev is not None:
        in_specs += [pl.BlockSpec(memory_space=pl.ANY)] * 3
        aliases = {len(args) + k: k for k in range(3)}
        args += list(prev)
    return pl.pallas_call(
        functools.partial(_norm2_kernel, n_real=n_real),
        grid=(n_real + n_zero,),
        in_specs=in_specs,
        out_specs=[pl.BlockSpec((tm, d), lambda i: (i + off, 0)),
                   pl.BlockSpec((tm, V7X_LANES), lambda i: (i + off, 0)),
                   pl.BlockSpec((tm, V7X_LANES), lambda i: (i + off, 0))],
        out_shape=[jax.ShapeDtypeStruct((n_total, d), F32),
                   jax.ShapeDtypeStruct((n_total, V7X_LANES), jnp.int32),
                   jax.ShapeDtypeStruct((n_total, V7X_LANES), F32)],
        input_output_aliases=aliases,
        compiler_params=_params(("parallel",), _vmem_limit(
            [_nbytes((tm, d), F32) * (2 if rows == 1 else 4)],
            temp_bytes=3 * _nbytes((tm, d), F32))),
        name="norm2_router",
    )(*args)


def _proj_kernel(a_ref, w_ref, *o_refs, scale, want_f32, want_bf16):
    acc = jnp.dot(a_ref[...], w_ref[...], preferred_element_type=F32)
    k = 0
    if want_f32:
        o_refs[k][...] = acc
        k += 1
    if want_bf16:
        o_refs[k][...] = (acc * scale).astype(BF16)


def _proj(a, w, col0, n, tm, tn, *, scale=1.0, want_f32, want_bf16):
    m, kd = a.shape
    cb0 = col0 // tn
    out_specs, out_shape, ob = [], [], []
    if want_f32:
        out_specs.append(pl.BlockSpec((tm, tn), lambda i, j: (i, j)))
        out_shape.append(jax.ShapeDtypeStruct((m, n), F32))
        ob.append(_nbytes((tm, tn), F32))
    if want_bf16:
        out_specs.append(pl.BlockSpec((tm, tn), lambda i, j: (i, j)))
        out_shape.append(jax.ShapeDtypeStruct((m, n), BF16))
        ob.append(_nbytes((tm, tn), BF16))
    return pl.pallas_call(
        functools.partial(_proj_kernel, scale=scale, want_f32=want_f32, want_bf16=want_bf16),
        grid=(m // tm, n // tn),
        in_specs=[pl.BlockSpec((tm, kd), lambda i, j: (i, 0)),
                  pl.BlockSpec((kd, tn), lambda i, j: (0, cb0 + j))],
        out_specs=out_specs,
        out_shape=out_shape,
        compiler_params=_params(("parallel", "arbitrary"), _vmem_limit(
            [_nbytes((tm, kd), BF16), _nbytes((kd, tn), BF16)] + ob,
            temp_bytes=_nbytes((tm, tn), F32))),
        name="proj",
    )(a, w)


def _merge_kernel(osb_ref, ofx_ref, h_ref, wsb_ref, wfx_ref, wg1_ref, wg2_ref, bg1_ref, bg2_ref, o_ref):
    h = h_ref[...]
    bs = jnp.dot(osb_ref[...], wsb_ref[...], preferred_element_type=F32)
    bx = jnp.dot(ofx_ref[...], wfx_ref[...], preferred_element_type=F32)
    g1 = jax.nn.sigmoid(jnp.dot(h, wg1_ref[...], preferred_element_type=F32) + bg1_ref[...])
    g2 = jax.nn.sigmoid(jnp.dot(h, wg2_ref[...], preferred_element_type=F32) + bg2_ref[...])
    o_ref[...] = (g1 * bs + g2 * bx).astype(BF16)


def _merge(o_sb, o_fx, h, w_bsb, w_bfx, w_gate, b_gate, tm, tn):
    m, d = h.shape
    ws = o_sb.shape[1]
    wx = o_fx.shape[1]
    nb = d // tn
    return pl.pallas_call(
        _merge_kernel,
        grid=(m // tm, nb),
        in_specs=[pl.BlockSpec((tm, ws), lambda i, j: (i, 0)),
                  pl.BlockSpec((tm, wx), lambda i, j: (i, 0)),
                  pl.BlockSpec((tm, d), lambda i, j: (i, 0)),
                  pl.BlockSpec((ws, tn), lambda i, j: (0, j)),
                  pl.BlockSpec((wx, tn), lambda i, j: (0, j)),
                  pl.BlockSpec((d, tn), lambda i, j: (0, j)),
                  pl.BlockSpec((d, tn), lambda i, j: (0, nb + j)),
                  pl.BlockSpec((1, tn), lambda i, j: (0, j)),
                  pl.BlockSpec((1, tn), lambda i, j: (0, nb + j))],
        out_specs=pl.BlockSpec((tm, tn), lambda i, j: (i, j)),
        out_shape=jax.ShapeDtypeStruct((m, d), BF16),
        compiler_params=_params(("parallel", "arbitrary"), _vmem_limit(
            [_nbytes((tm, ws + wx + d), BF16), _nbytes((ws + wx + 2 * d, tn), BF16), _nbytes((tm, tn), BF16)],
            temp_bytes=6 * _nbytes((tm, tn), F32))),
        name="merge",
    )(o_sb, o_fx, h, w_bsb, w_bfx, w_gate, w_gate, b_gate, b_gate)


def _outproj_kernel(a_ref, w_ref, x_ref, g_ref, o_ref):
    acc = jnp.dot(a_ref[...], w_ref[...], preferred_element_type=F32)
    o_ref[...] = x_ref[...] + g_ref[...] * acc


def _outproj(a, w, x, gate, tm, tn):
    m, kd = a.shape
    n = w.shape[1]
    rows = gate.shape[0]
    g_spec = pl.BlockSpec((1, tn), lambda i, j: (0, j)) if rows == 1 else pl.BlockSpec((tm, tn), lambda i, j: (i, j))
    return pl.pallas_call(
        _outproj_kernel,
        grid=(m // tm, n // tn),
        in_specs=[pl.BlockSpec((tm, kd), lambda i, j: (i, 0)),
                  pl.BlockSpec((kd, tn), lambda i, j: (0, j)),
                  pl.BlockSpec((tm, tn), lambda i, j: (i, j)),
                  g_spec],
        out_specs=pl.BlockSpec((tm, tn), lambda i, j: (i, j)),
        out_shape=jax.ShapeDtypeStruct((m, n), F32),
        compiler_params=_params(("parallel", "arbitrary"), _vmem_limit(
            [_nbytes((tm, kd), BF16), _nbytes((kd, tn), BF16), 3 * _nbytes((tm, tn), F32)],
            temp_bytes=_nbytes((tm, tn), F32))),
        name="outproj",
    )(a, w, x, gate)


def _cumsum_kernel(lf_ref, tri_ref, o_ref, *, chunk):
    t = lf_ref.shape[1]
    nh = lf_ref.shape[2]
    carry = jnp.zeros((1, nh), F32)
    start = 0
    while start < t:
        c = min(chunk, t - start)
        x = lf_ref[0, start:start + c, :]
        tri = tri_ref[:c, :c]
        hi, mid, lo = _split3(x)
        s = (jnp.dot(tri, hi, preferred_element_type=F32)
             + jnp.dot(tri, mid, preferred_element_type=F32)
             + jnp.dot(tri, lo, preferred_element_type=F32)) + carry
        o_ref[0, start:start + c, :] = s
        carry = s[c - 1:c, :]
        start += c


def _cumsum_time(lf, tri_incl, chunk):
    b, t, nh = lf.shape
    return pl.pallas_call(
        functools.partial(_cumsum_kernel, chunk=chunk),
        grid=(b,),
        in_specs=[pl.BlockSpec((1, t, nh), lambda i: (i, 0, 0)),
                  pl.BlockSpec((chunk, chunk), lambda i: (0, 0))],
        out_specs=pl.BlockSpec((1, t, nh), lambda i: (i, 0, 0)),
        out_shape=jax.ShapeDtypeStruct((b, t, nh), F32),
        compiler_params=_params(("parallel",), _vmem_limit(
            [2 * _nbytes((t, V7X_LANES), F32)], temp_bytes=4 << 20)),
        name="cumsum_logf",
    )(lf, tri_incl)


def _qk(q, k):
    if q.ndim == 3:
        return jnp.einsum("hqd,hkd->hqk", q, k, preferred_element_type=F32)
    return lax.dot_general(q, k, (((1,), (1,)), ((), ())), preferred_element_type=F32)


def _pv(p, v):
    if p.ndim == 3:
        return jnp.einsum("hqk,hkd->hqd", p, v, preferred_element_type=F32)
    return jnp.dot(p, v, preferred_element_type=F32)


def _rows_dot(x, m):
    if x.ndim == 3:
        h, r, n = x.shape
        return jnp.dot(x.reshape(h * r, n), m, preferred_element_type=F32).reshape(h, r, n)
    return jnp.dot(x, m, preferred_element_type=F32)


def _sb_step(q, k, v, carry, acc, tri, mask):
    hw = tri.shape[0]
    groups = k.shape[-2] // hw
    z = _qk(q, k)
    ls = -(jnp.maximum(z, 0.0) + jnp.log(1.0 + jnp.exp2(-jnp.abs(z))) * LOG2E)
    lsm = ls if mask is None else jnp.where(mask, ls, 0.0)
    hi = lsm.astype(BF16)
    lo = (lsm - hi.astype(F32)).astype(BF16)
    lw = z + ls
    parts = [None] * groups
    for g in reversed(range(groups)):
        c0 = g * hw
        cs = _rows_dot(hi[..., c0:c0 + hw], tri) + _rows_dot(lo[..., c0:c0 + hw], tri)
        parts[g] = jnp.exp2(lw[..., c0:c0 + hw] + cs + carry)
        carry = carry + cs[..., :1] + lsm[..., c0:c0 + 1]
    w = parts[0] if groups == 1 else jnp.concatenate(parts, axis=-1)
    if mask is not None:
        w = jnp.where(mask, w, 0.0)
    acc = acc + _pv(w.astype(BF16), v)
    return carry, acc


def _strict_lower_mask(tq, tk):
    row = lax.broadcasted_iota(jnp.int32, (tq, tk), 0)
    col = lax.broadcasted_iota(jnp.int32, (tq, tk), 1)
    return col < row


def _sb_prompt_kernel(q_ref, k_ref, v_ref, tri_ref, o_ref, *, tq):
    i = pl.program_id(1)
    q = q_ref[...]
    tri = tri_ref[...]
    d0 = pl.multiple_of(i * tq, tq)
    carry = jnp.zeros((tq, 1), F32)
    acc = jnp.zeros((tq, HEAD_DIM), F32)
    carry, acc = _sb_step(q, k_ref[pl.ds(d0, tq), :], v_ref[pl.ds(d0, tq), :], carry, acc, tri,
                          _strict_lower_mask(tq, tq))

    def body(n, c):
        r0 = pl.multiple_of((i - 1 - n) * tq, tq)
        return _sb_step(q, k_ref[pl.ds(r0, tq), :], v_ref[pl.ds(r0, tq), :], c[0], c[1], tri, None)

    carry, acc = lax.fori_loop(0, i, body, (carry, acc))
    o_ref[...] = acc.astype(BF16)


def _sb_prompt(q, k, v, tri, n_heads, tq):
    t = q.shape[0]
    return pl.pallas_call(
        functools.partial(_sb_prompt_kernel, tq=tq),
        grid=(n_heads, t // tq),
        in_specs=[pl.BlockSpec((tq, HEAD_DIM), lambda h, i: (i, h)),
                  pl.BlockSpec((t, HEAD_DIM), lambda h, i: (0, h)),
                  pl.BlockSpec((t, HEAD_DIM), lambda h, i: (0, h)),
                  pl.BlockSpec(tri.shape, lambda h, i: (0, 0))],
        out_specs=pl.BlockSpec((tq, HEAD_DIM), lambda h, i: (i, h)),
        out_shape=jax.ShapeDtypeStruct((t, n_heads * HEAD_DIM), BF16),
        compiler_params=_params(("parallel", "arbitrary"), _vmem_limit(
            [2 * _nbytes((t, HEAD_DIM), BF16), _nbytes(tri.shape, BF16)],
            temp_bytes=14 * _nbytes((tq, tq), F32))),
        name="sb_prompt",
    )(q, k, v, tri)


def _past_heads(ref, n_heads, tk):
    return jnp.stack([ref[pl.ds(h, tk, stride=n_heads), :] for h in range(n_heads)]).astype(BF16)


def _new_heads(ref, n_heads):
    return jnp.stack([ref[0, :, h * HEAD_DIM:(h + 1) * HEAD_DIM] for h in range(n_heads)])


def _sb_sample_kernel(q_ref, kn_ref, vn_ref, kp_ref, vp_ref, tri_ref, trin_ref, o_ref, carry_s, acc_s,
                      *, n_heads, tk):
    j = pl.program_id(1)
    tq = q_ref.shape[1]
    q = _new_heads(q_ref, n_heads)

    @pl.when(j == 0)
    def _():
        c, a = _sb_step(q, _new_heads(kn_ref, n_heads), _new_heads(vn_ref, n_heads),
                        jnp.zeros((n_heads, tq, 1), F32), jnp.zeros((n_heads, tq, HEAD_DIM), F32),
                        trin_ref[...], _strict_lower_mask(tq, tq))
        carry_s[...] = jnp.broadcast_to(c, carry_s.shape)
        acc_s[...] = a

    c, a = _sb_step(q, _past_heads(kp_ref, n_heads, tk), _past_heads(vp_ref, n_heads, tk),
                    carry_s[:, :, :1], acc_s[...], tri_ref[...], None)
    carry_s[...] = jnp.broadcast_to(c, carry_s.shape)
    acc_s[...] = a

    @pl.when(j == pl.num_programs(1) - 1)
    def _():
        for h in range(n_heads):
            o_ref[0, :, h * HEAD_DIM:(h + 1) * HEAD_DIM] = acc_s[h].astype(BF16)


def _sb_sample(q, k_new, v_new, k_past, v_past, tri, tri_new, n_heads, tk):
    b, tq, width = q.shape
    p = k_past.shape[0] // (b * n_heads)
    nc = p // tk
    new_spec = pl.BlockSpec((1, tq, width), lambda bi, j: (bi, 0, 0))
    past_spec = pl.BlockSpec((tk * n_heads, HEAD_DIM), lambda bi, j: (bi * nc + (nc - 1 - j), 0))
    return pl.pallas_call(
        functools.partial(_sb_sample_kernel, n_heads=n_heads, tk=tk),
        grid=(b, nc),
        in_specs=[new_spec, new_spec, new_spec, past_spec, past_spec,
                  pl.BlockSpec(tri.shape, lambda bi, j: (0, 0)),
                  pl.BlockSpec(tri_new.shape, lambda bi, j: (0, 0))],
        out_specs=new_spec,
        out_shape=jax.ShapeDtypeStruct((b, tq, width), BF16),
        scratch_shapes=[pltpu.VMEM((n_heads, tq, V7X_LANES), F32), pltpu.VMEM((n_heads, tq, HEAD_DIM), F32)],
        compiler_params=_params(("parallel", "arbitrary"), _vmem_limit(
            [2 * _nbytes((tk * n_heads, HEAD_DIM), F32), 4 * _nbytes((tq, width), BF16)],
            scratch_bytes=2 * _nbytes((n_heads, tq, V7X_LANES), F32), temp_bytes=8 << 20)),
        name="sb_sample",
    )(q, k_new, v_new, k_past, v_past, tri, tri_new)


def _fox_step(q, k, v, fk_row, fq_col, m, l, acc, mask):
    s = _qk(q, k) - fk_row
    if mask is not None:
        s = jnp.where(mask, s, NEG_BIG)
    m_new = jnp.maximum(m, jnp.max(s, axis=-1, keepdims=True) + fq_col)
    pexp = jnp.exp2(s + (fq_col - m_new))
    alpha = jnp.exp2(m - m_new)
    l = alpha * l + jnp.sum(pexp, axis=-1, keepdims=True)
    acc = alpha * acc + _pv(pexp.astype(BF16), v)
    return m_new, l, acc


def _lower_incl_mask(tq, tk):
    row = lax.broadcasted_iota(jnp.int32, (tq, tk), 0)
    col = lax.broadcasted_iota(jnp.int32, (tq, tk), 1)
    return col <= row


def _pick_head_column(f_cols, h):
    lane = lax.broadcasted_iota(jnp.int32, f_cols.shape, 1)
    return jnp.sum(jnp.where(lane == h, f_cols, 0.0), axis=1, keepdims=True)


def _fox_prompt_kernel(q_ref, k_ref, v_ref, fr_ref, fc_ref, o_ref, *, tq):
    h = pl.program_id(0)
    i = pl.program_id(1)
    q = q_ref[...]
    fq = _pick_head_column(fc_ref[...], h) * LOG2E
    d0 = pl.multiple_of(i * tq, tq)
    m = jnp.full((tq, 1), NEG_BIG, F32)
    l = jnp.zeros((tq, 1), F32)
    acc = jnp.zeros((tq, HEAD_DIM), F32)
    m, l, acc = _fox_step(q, k_ref[pl.ds(d0, tq), :], v_ref[pl.ds(d0, tq), :],
                          fr_ref[0, :, pl.ds(d0, tq)] * LOG2E, fq, m, l, acc, _lower_incl_mask(tq, tq))

    def body(n, c):
        r0 = pl.multiple_of(n * tq, tq)
        return _fox_step(q, k_ref[pl.ds(r0, tq), :], v_ref[pl.ds(r0, tq), :],
                         fr_ref[0, :, pl.ds(r0, tq)] * LOG2E, fq, c[0], c[1], c[2], None)

    m, l, acc = lax.fori_loop(0, i, body, (m, l, acc))
    o_ref[...] = (acc / l).astype(BF16)


def _fox_prompt(q, k, v, f_rows, f_cols, n_heads, tq):
    t = q.shape[0]
    return pl.pallas_call(
        functools.partial(_fox_prompt_kernel, tq=tq),
        grid=(n_heads, t // tq),
        in_specs=[pl.BlockSpec((tq, HEAD_DIM), lambda h, i: (i, h)),
                  pl.BlockSpec((t, HEAD_DIM), lambda h, i: (0, h)),
                  pl.BlockSpec((t, HEAD_DIM), lambda h, i: (0, h)),
                  pl.BlockSpec((1, 1, t), lambda h, i: (h, 0, 0)),
                  pl.BlockSpec((tq, n_heads), lambda h, i: (i, 0))],
        out_specs=pl.BlockSpec((tq, HEAD_DIM), lambda h, i: (i, h)),
        out_shape=jax.ShapeDtypeStruct((t, n_heads * HEAD_DIM), BF16),
        compiler_params=_params(("parallel", "arbitrary"), _vmem_limit(
            [2 * _nbytes((t, HEAD_DIM), BF16), _nbytes((8, t), F32)],
            temp_bytes=10 * _nbytes((tq, tq), F32))),
        name="fox_prompt",
    )(q, k, v, f_rows, f_cols)


def _fox_sample_kernel(q_ref, kn_ref, vn_ref, kp_ref, vp_ref, frp_ref, frn_ref, fc_ref, o_ref, m_s, l_s, acc_s,
                       *, n_heads, tk):
    j = pl.program_id(1)
    tq = q_ref.shape[1]
    q = _new_heads(q_ref, n_heads)
    fc = fc_ref[0] * LOG2E
    fq = jnp.stack([fc[:, h:h + 1] for h in range(n_heads)])

    @pl.when(j == 0)
    def _():
        m, l, a = _fox_step(q, _new_heads(kn_ref, n_heads), _new_heads(vn_ref, n_heads),
                            (frn_ref[0] * LOG2E)[:, None, :], fq,
                            jnp.full((n_heads, tq, 1), NEG_BIG, F32), jnp.zeros((n_heads, tq, 1), F32),
                            jnp.zeros((n_heads, tq, HEAD_DIM), F32), _lower_incl_mask(tq, tq))
        m_s[...] = jnp.broadcast_to(m, m_s.shape)
        l_s[...] = jnp.broadcast_to(l, l_s.shape)
        acc_s[...] = a

    m, l, a = _fox_step(q, _past_heads(kp_ref, n_heads, tk), _past_heads(vp_ref, n_heads, tk),
                        (frp_ref[0] * LOG2E)[:, None, :], fq,
                        m_s[:, :, :1], l_s[:, :, :1], acc_s[...], None)
    m_s[...] = jnp.broadcast_to(m, m_s.shape)
    l_s[...] = jnp.broadcast_to(l, l_s.shape)
    acc_s[...] = a

    @pl.when(j == pl.num_programs(1) - 1)
    def _():
        for h in range(n_heads):
            o_ref[0, :, h * HEAD_DIM:(h + 1) * HEAD_DIM] = (acc_s[h] / l_s[h][:, :1]).astype(BF16)


def _fox_sample(q, k_new, v_new, k_past, v_past, f_rows_past, f_rows_new, f_cols_new, n_heads, tk):
    b, tq, width = q.shape
    p = f_rows_past.shape[2]
    nc = p // tk
    new_spec = pl.BlockSpec((1, tq, width), lambda bi, j: (bi, 0, 0))
    past_spec = pl.BlockSpec((tk * n_heads, HEAD_DIM), lambda bi, j: (bi * nc + j, 0))
    return pl.pallas_call(
        functools.partial(_fox_sample_kernel, n_heads=n_heads, tk=tk),
        grid=(b, nc),
        in_specs=[new_spec, new_spec, new_spec, past_spec, past_spec,
                  pl.BlockSpec((1, n_heads, tk), lambda bi, j: (bi, 0, j)),
                  pl.BlockSpec((1, n_heads, tq), lambda bi, j: (bi, 0, 0)),
                  pl.BlockSpec((1, tq, n_heads), lambda bi, j: (bi, 0, 0))],
        out_specs=new_spec,
        out_shape=jax.ShapeDtypeStruct((b, tq, width), BF16),
        scratch_shapes=[pltpu.VMEM((n_heads, tq, V7X_LANES), F32), pltpu.VMEM((n_heads, tq, V7X_LANES), F32),
                        pltpu.VMEM((n_heads, tq, HEAD_DIM), F32)],
        compiler_params=_params(("parallel", "arbitrary"), _vmem_limit(
            [2 * _nbytes((tk * n_heads, HEAD_DIM), F32), 4 * _nbytes((tq, width), BF16)],
            scratch_bytes=3 * _nbytes((n_heads, tq, V7X_LANES), F32), temp_bytes=8 << 20)),
        name="fox_sample",
    )(q, k_new, v_new, k_past, v_past, f_rows_past, f_rows_new, f_cols_new)


def _row_copy(src_hbm, dst, row_src, row_dst, sem):
    return pltpu.make_async_copy(src_hbm.at[pl.ds(row_src, 1)], dst.at[pl.ds(row_dst, 1)], sem)


def _gather_kernel(nv_ref, cur_ref, nxt_ref, x_hbm, o_ref, buf, sem, *, tg):
    i = pl.program_id(0)
    nv = nv_ref[0]

    def issue(idx_ref, slot):
        def body(r, _):
            _row_copy(x_hbm, buf.at[slot], idx_ref[r], r, sem.at[slot]).start()
            return 0

        lax.fori_loop(0, tg, body, 0, unroll=8)

    @pl.when(i == 0)
    def _():
        issue(cur_ref, 0)

    @pl.when(i + 1 < nv)
    def _():
        issue(nxt_ref, (i + 1) % 2)

    @pl.when(i < nv)
    def _():
        slot = i % 2
        pltpu.make_async_copy(x_hbm.at[pl.ds(0, tg)], buf.at[slot], sem.at[slot]).wait()
        o_ref[...] = buf[slot].astype(BF16)

    @pl.when(i >= nv)
    def _():
        o_ref[...] = jnp.zeros(o_ref.shape, BF16)


def _gather_tokens(n_valid_tiles, src, x, n_rows, tg):
    d = x.shape[1]
    nt = n_rows // tg
    return pl.pallas_call(
        functools.partial(_gather_kernel, tg=tg),
        grid_spec=pltpu.PrefetchScalarGridSpec(
            num_scalar_prefetch=1,
            grid=(nt,),
            in_specs=[pl.BlockSpec((tg,), lambda i, nv: (jnp.minimum(i, nv[0] - 1),), memory_space=pltpu.SMEM),
                      pl.BlockSpec((tg,), lambda i, nv: (jnp.minimum(i + 1, nv[0] - 1),), memory_space=pltpu.SMEM),
                      pl.BlockSpec(memory_space=pl.ANY)],
            out_specs=pl.BlockSpec((tg, d), lambda i, nv: (i, 0)),
            scratch_shapes=[pltpu.VMEM((2, tg, d), F32), pltpu.SemaphoreType.DMA((2,))]),
        out_shape=jax.ShapeDtypeStruct((n_rows, d), BF16),
        compiler_params=_params(("arbitrary",), _vmem_limit(
            [_nbytes((tg, d), BF16)], scratch_bytes=_nbytes((2, tg, d), F32), temp_bytes=_nbytes((tg, d), F32))),
        name="gather_tokens",
    )(n_valid_tiles, src, src, x)


def _tile_meta(nv_ref, te_ref, tv_ref):
    t = pl.program_id(1)
    active = t < nv_ref[0]
    prev = te_ref[jnp.maximum(t - 1, 0)]
    first = jnp.logical_and(active, jnp.logical_or(t == 0, te_ref[t] != prev))
    return active, first, tv_ref[t]


def _moe_up_kernel(nv_ref, te_ref, tv_ref, x_ref, wg_ref, wu_ref, bg_ref, bu_ref, a_ref, wg_bf, wu_bf, *, sub):
    active, first, valid = _tile_meta(nv_ref, te_ref, tv_ref)

    @pl.when(first)
    def _():
        wg_bf[...] = wg_ref[0].astype(BF16)
        wu_bf[...] = wu_ref[0].astype(BF16)

    tm = x_ref.shape[0]
    for s in range(tm // sub):
        rows = pl.ds(s * sub, sub)
        has_rows = (s + 1) * sub > tm - valid

        @pl.when(jnp.logical_and(active, has_rows))
        def _():
            x = x_ref[rows, :]
            gate = jnp.minimum(jnp.dot(x, wg_bf[...], preferred_element_type=F32) + bg_ref[0], SWIGLU_LIMIT)
            up = jnp.clip(jnp.dot(x, wu_bf[...], preferred_element_type=F32) + bu_ref[0],
                          -SWIGLU_LIMIT, SWIGLU_LIMIT)
            a_ref[rows, :] = (gate * jax.nn.sigmoid(SWIGLU_ALPHA * gate) * (up + 1.0)).astype(BF16)

        @pl.when(jnp.logical_not(jnp.logical_and(active, has_rows)))
        def _():
            a_ref[rows, :] = jnp.zeros((sub, a_ref.shape[1]), BF16)


def _clamped_tile(t, nv):
    return jnp.minimum(t, nv[0] - 1)


def _moe_up(n_valid_tiles, tile_expert, tile_valid, x_sorted, w_gate, b_gate, w_up, b_up, tm, tn, sub):
    r, d = x_sorted.shape
    n_exp, _, f = w_gate.shape
    w_spec = pl.BlockSpec((1, d, tn), lambda n, t, nv, te, tv: (te[_clamped_tile(t, nv)], 0, n))
    b_spec = pl.BlockSpec((1, 1, tn), lambda n, t, nv, te, tv: (te[_clamped_tile(t, nv)], 0, n))
    return pl.pallas_call(
        functools.partial(_moe_up_kernel, sub=sub),
        grid_spec=pltpu.PrefetchScalarGridSpec(
            num_scalar_prefetch=3,
            grid=(f // tn, r // tm),
            in_specs=[pl.BlockSpec((tm, d), lambda n, t, nv, te, tv: (_clamped_tile(t, nv), 0)),
                      w_spec, w_spec, b_spec, b_spec],
            out_specs=pl.BlockSpec((tm, tn), lambda n, t, nv, te, tv: (t, n)),
            scratch_shapes=[pltpu.VMEM((d, tn), BF16), pltpu.VMEM((d, tn), BF16)]),
        out_shape=jax.ShapeDtypeStruct((r, f), BF16),
        compiler_params=_params(("arbitrary", "arbitrary"), _vmem_limit(
            [_nbytes((tm, d), BF16), 2 * _nbytes((d, tn), F32), _nbytes((tm, tn), BF16)],
            scratch_bytes=2 * _nbytes((d, tn), BF16), temp_bytes=6 * _nbytes((sub, tn), F32))),
        name="moe_up",
    )(n_valid_tiles, tile_expert, tile_valid, x_sorted, w_gate, w_up,
      b_gate.reshape(n_exp, 1, f), b_up.reshape(n_exp, 1, f))


def _moe_down_kernel(nv_ref, te_ref, tv_ref, a_ref, wd_ref, bd_ref, cw_ref, y_ref, wd_bf, *, sub):
    active, first, valid = _tile_meta(nv_ref, te_ref, tv_ref)

    @pl.when(first)
    def _():
        wd_bf[...] = wd_ref[0].astype(BF16)

    tm = a_ref.shape[0]
    for s in range(tm // sub):
        rows = pl.ds(s * sub, sub)
        has_rows = (s + 1) * sub > tm - valid

        @pl.when(jnp.logical_and(active, has_rows))
        def _():
            y = jnp.dot(a_ref[rows, :], wd_bf[...], preferred_element_type=F32) + bd_ref[0]
            y_ref[rows, :] = cw_ref[rows, :] * y

        @pl.when(jnp.logical_not(jnp.logical_and(active, has_rows)))
        def _():
            y_ref[rows, :] = jnp.zeros((sub, y_ref.shape[1]), F32)


def _moe_down(n_valid_tiles, tile_expert, tile_valid, act, w_down, b_down, cw_sorted, tm, tn, sub):
    r, f = act.shape
    n_exp, _, d = w_down.shape
    return pl.pallas_call(
        functools.partial(_moe_down_kernel, sub=sub),
        grid_spec=pltpu.PrefetchScalarGridSpec(
            num_scalar_prefetch=3,
            grid=(d // tn, r // tm),
            in_specs=[pl.BlockSpec((tm, f), lambda n, t, nv, te, tv: (_clamped_tile(t, nv), 0)),
                      pl.BlockSpec((1, f, tn), lambda n, t, nv, te, tv: (te[_clamped_tile(t, nv)], 0, n)),
                      pl.BlockSpec((1, 1, tn), lambda n, t, nv, te, tv: (te[_clamped_tile(t, nv)], 0, n)),
                      pl.BlockSpec((tm, 1), lambda n, t, nv, te, tv: (_clamped_tile(t, nv), 0))],
            out_specs=pl.BlockSpec((tm, tn), lambda n, t, nv, te, tv: (t, n)),
            scratch_shapes=[pltpu.VMEM((f, tn), BF16)]),
        out_shape=jax.ShapeDtypeStruct((r, d), F32),
        compiler_params=_params(("arbitrary", "arbitrary"), _vmem_limit(
            [_nbytes((tm, f), BF16), _nbytes((f, tn), F32), _nbytes((tm, tn), F32),
             _nbytes((tm, V7X_LANES), F32)],
            scratch_bytes=_nbytes((f, tn), BF16), temp_bytes=3 * _nbytes((sub, tn), F32))),
        name="moe_down",
    )(n_valid_tiles, tile_expert, tile_valid, act, w_down, b_down.reshape(n_exp, 1, d), cw_sorted)


def _combine_kernel(cur_ref, nxt_ref, y_hbm, xp_ref, xs_ref, gp_ref, gs_ref, lnf_ref, op_ref, os_ref, buf, sem,
                    *, n_tiles_p, tc):
    i = pl.program_id(0)
    n = pl.num_programs(0)

    def issue(idx_ref, slot):
        def body(r, _):
            for k in range(TOP_K):
                _row_copy(y_hbm, buf.at[slot, k], idx_ref[r * TOP_K + k], r, sem.at[slot]).start()
            return 0

        lax.fori_loop(0, tc, body, 0, unroll=4)

    @pl.when(i == 0)
    def _():
        issue(cur_ref, 0)

    @pl.when(i + 1 < n)
    def _():
        issue(nxt_ref, (i + 1) % 2)

    slot = i % 2
    for k in range(TOP_K):
        pltpu.make_async_copy(y_hbm.at[pl.ds(0, tc)], buf.at[slot, k], sem.at[slot]).wait()
    ffn = (buf[slot, 0] + buf[slot, 1]) + (buf[slot, 2] + buf[slot, 3])

    def finish(x, g):
        x2 = x + g * ffn
        ms = jnp.mean(x2 * x2, axis=-1, keepdims=True)
        return x2 * lax.rsqrt(ms + EPS) * lnf_ref[...]

    @pl.when(i < n_tiles_p)
    def _():
        op_ref[...] = finish(xp_ref[...], gp_ref[...])

    @pl.when(i >= n_tiles_p)
    def _():
        os_ref[...] = finish(xs_ref[...], gs_ref[...])


def _combine(pos_flat, y_sorted, x1_p, x1_s, gm_p, gm_s, ln_final_g, tc):
    n_p, d = x1_p.shape
    n_s = x1_s.shape[0]
    ntp = n_p // tc
    nts = n_s // tc
    nt = ntp + nts
    p_map = lambda i: (jnp.minimum(i, ntp - 1), 0)
    s_map = lambda i: (jnp.maximum(i - ntp, 0), 0)
    tp = tc * TOP_K
    return pl.pallas_call(
        functools.partial(_combine_kernel, n_tiles_p=ntp, tc=tc),
        grid=(nt,),
        in_specs=[pl.BlockSpec((tp,), lambda i: (i,), memory_space=pltpu.SMEM),
                  pl.BlockSpec((tp,), lambda i: (jnp.minimum(i + 1, nt - 1),), memory_space=pltpu.SMEM),
                  pl.BlockSpec(memory_space=pl.ANY),
                  pl.BlockSpec((tc, d), p_map),
                  pl.BlockSpec((tc, d), s_map),
                  pl.BlockSpec((1, d), lambda i: (0, 0)),
                  pl.BlockSpec((tc, d), s_map),
                  pl.BlockSpec((1, d), lambda i: (0, 0))],
        out_specs=[pl.BlockSpec((tc, d), p_map), pl.BlockSpec((tc, d), s_map)],
        scratch_shapes=[pltpu.VMEM((2, TOP_K, tc, d), F32), pltpu.SemaphoreType.DMA((2,))],
        out_shape=[jax.ShapeDtypeStruct((n_p, d), F32), jax.ShapeDtypeStruct((n_s, d), F32)],
        compiler_params=_params(("arbitrary",), _vmem_limit(
            [5 * _nbytes((tc, d), F32)], scratch_bytes=_nbytes((2, TOP_K, tc, d), F32),
            temp_bytes=3 * _nbytes((tc, d), F32))),
        name="combine_final",
    )(pos_flat, pos_flat, y_sorted, x1_p, x1_s, gm_p, gm_s, ln_final_g)


def _route_metadata(top_idx, n_exp, tm):
    n_tok, k = top_idx.shape
    n_pairs = n_tok * k
    e_flat = top_idx.reshape(n_pairs)
    onehot = (e_flat[:, None] == jnp.arange(n_exp, dtype=jnp.int32)[None, :]).astype(jnp.int32)
    counts = jnp.sum(onehot, axis=0)
    rank = jnp.sum((jnp.cumsum(onehot, axis=0) - 1) * onehot, axis=1)
    padded = ((counts + tm - 1) // tm) * tm
    ends = jnp.cumsum(padded)
    first_row = ends - counts
    pos = (first_row[e_flat] + rank).astype(jnp.int32)
    n_tiles = n_pairs // tm + n_exp
    n_rows = n_tiles * tm
    tok = (jnp.arange(n_pairs, dtype=jnp.int32) // k)
    src = jnp.zeros((n_rows,), jnp.int32).at[pos].set(tok)
    tile_start = jnp.arange(n_tiles, dtype=jnp.int32) * tm
    tile_expert = jnp.minimum(jnp.searchsorted(ends, tile_start, side="right"), n_exp - 1).astype(jnp.int32)
    tile_valid = jnp.clip(tile_start + tm - first_row[tile_expert], 0, tm).astype(jnp.int32)
    n_valid_tiles = (ends[n_exp - 1] // tm).astype(jnp.int32).reshape(1)
    return pos, src, tile_expert, tile_valid, n_valid_tiles, n_rows


def _tri_strict(n):
    j = jnp.arange(n)[:, None]
    s = jnp.arange(n)[None, :]
    return (j > s).astype(BF16)


def _tri_incl(n):
    t = jnp.arange(n)[:, None]
    s = jnp.arange(n)[None, :]
    return (s <= t).astype(BF16)


def kernel(x_prompt, x_sample, cache_sb_k, cache_sb_v, cache_fox_k, cache_fox_v, cache_fox_logf, c_prompt, c_sample, ln_attn_g, ln_moe_g, w_ada, b_ada, w_in, b_forget, w_branch_sb, w_branch_fox, w_merge_gate, b_merge_gate, w_out, w_router, b_router, w_exp_gate, b_exp_gate, w_exp_up, b_exp_up, w_exp_down, b_exp_down, ln_final_g):
    bp, seq, d = x_prompt.shape
    bs, dseq, _ = x_sample.shape
    depth = w_in.shape[0]
    assert depth == 1 and bp == 1
    n_heads = b_forget.shape[1]
    width = n_heads * HEAD_DIM
    n_exp = w_router.shape[2]
    past = cache_sb_k.shape[2]
    n_p = bp * seq
    n_s = bs * dseq
    n_tok = n_p + n_s
    q_scale = HEAD_DIM ** -0.5 * LOG2E

    c_all = jnp.concatenate([c_prompt, c_sample], axis=0)
    pad = (-c_all.shape[0]) % 8
    c_all = jnp.pad(c_all, ((0, pad), (0, 0)))
    mod = _ada_mod(c_all, w_ada[0], b_ada[0])
    mod_p = jnp.split(mod[0:1], N_MOD, axis=-1)
    mod_s = [jnp.repeat(m_, dseq, axis=0) for m_ in jnp.split(mod[1:1 + bs], N_MOD, axis=-1)]

    w_in_b = w_in[0].astype(BF16)
    wf_pad = jnp.pad(w_in_b[:, 6 * width:6 * width + n_heads], ((0, 0), (0, V7X_LANES - n_heads)))
    w_bsb = w_branch_sb[0].astype(BF16)
    w_bfx = w_branch_fox[0].astype(BF16)
    w_gate = w_merge_gate[0].astype(BF16)
    b_gate = b_merge_gate[0].reshape(1, 2 * d)
    w_o = w_out[0].astype(BF16)
    wr = jnp.pad(w_router[0], ((0, 0), (0, V7X_LANES - n_exp)))
    wr_hi = wr.astype(BF16)
    wr_lo = (wr - wr_hi.astype(F32)).astype(BF16)
    br_pad = jnp.pad(b_router[0].reshape(1, n_exp), ((0, 0), (0, V7X_LANES - n_exp)), constant_values=-jnp.inf)
    g_attn = ln_attn_g[0].reshape(1, d)
    g_moe = ln_moe_g[0].reshape(1, d)
    bf = b_forget[0].reshape(1, n_heads)

    tq = 2 * V7X_MXU_DIM
    tk_s = 2 * V7X_MXU_DIM
    tri_q = _tri_strict(V7X_MXU_DIM)
    tri_n = _tri_strict(dseq)
    tri_c = _tri_incl(V7X_MXU_DIM)

    def stream(x2d, mods, tm_norm, tm_mm, tn_mm):
        sh_a, sc_a, g_a = mods[0], mods[1], mods[2]
        h, lf = _norm1(x2d, g_attn, sc_a, sh_a, wf_pad, bf, tm_norm)
        pj = functools.partial(_proj, h, w_in_b, tm=tm_mm, tn=tn_mm)
        q_sb = pj(0 * width, width, scale=q_scale, want_f32=False, want_bf16=True)[0]
        k_sb, k_sb_b = pj(1 * width, width, want_f32=True, want_bf16=True)
        v_sb, v_sb_b = pj(2 * width, width, want_f32=True, want_bf16=True)
        q_fx = pj(3 * width, width, scale=q_scale, want_f32=False, want_bf16=True)[0]
        k_fx, k_fx_b = pj(4 * width, width, want_f32=True, want_bf16=True)
        v_fx, v_fx_b = pj(5 * width, width, want_f32=True, want_bf16=True)
        return h, lf, g_a, (q_sb, k_sb, v_sb, k_sb_b, v_sb_b), (q_fx, k_fx, v_fx, k_fx_b, v_fx_b)

    xp2 = x_prompt.reshape(n_p, d)
    h_p, lf_p, ga_p, sb_p, fx_p = stream(xp2, mod_p, 256, 1024, 512)
    o_sb_p = _sb_prompt(sb_p[0], sb_p[3], sb_p[4], tri_q, n_heads, tq)
    fcum_p = _cumsum_time(lf_p.reshape(1, n_p, n_heads), tri_c, V7X_MXU_DIM)[0]
    f_rows_p = fcum_p.T.reshape(n_heads, 1, n_p)
    o_fx_p = _fox_prompt(fx_p[0], fx_p[3], fx_p[4], f_rows_p, fcum_p, n_heads, tq)
    merged_p = _merge(o_sb_p, o_fx_p, h_p, w_bsb, w_bfx, w_gate, b_gate, 1024, 256)
    x1_p = _outproj(merged_p, w_o, xp2, ga_p, 1024, 512)

    xs2 = x_sample.reshape(n_s, d)
    h_s, lf_s, ga_s, sb_s, fx_s = stream(xs2, mod_s, n_s, n_s, 512)
    r3 = lambda a: a.reshape(bs, dseq, width)
    c2 = lambda a: a.reshape(bs * past * n_heads, HEAD_DIM)
    o_sb_s = _sb_sample(r3(sb_s[0]), r3(sb_s[3]), r3(sb_s[4]), c2(cache_sb_k), c2(cache_sb_v),
                        tri_q, tri_n, n_heads, tk_s)
    lf_all = jnp.concatenate([cache_fox_logf[0].astype(F32), lf_s.reshape(bs, dseq, n_heads)], axis=1)
    fcum_s = _cumsum_time(lf_all, tri_c, V7X_MXU_DIM)
    f_rows_s = jnp.swapaxes(fcum_s, 1, 2)
    o_fx_s = _fox_sample(r3(fx_s[0]), r3(fx_s[3]), r3(fx_s[4]), c2(cache_fox_k), c2(cache_fox_v),
                         f_rows_s[:, :, :past], f_rows_s[:, :, past:], fcum_s[:, past:, :], n_heads, tk_s)
    merged_s = _merge(o_sb_s.reshape(n_s, width), o_fx_s.reshape(n_s, width), h_s,
                      w_bsb, w_bfx, w_gate, b_gate, n_s, 256)
    x1_s = _outproj(merged_s, w_o, xs2, ga_s, n_s, 512)

    routed = _norm2_router(x1_p, g_moe, mod_p[4], mod_p[3], wr_hi, wr_lo, br_pad, 256, n_tok, 0)
    h2, idx, tw = _norm2_router(x1_s, g_moe, mod_s[4], mod_s[3], wr_hi, wr_lo, br_pad, n_s, n_tok, n_p, prev=routed)
    tm_moe, sub = 512, 128
    pos, src, tile_expert, tile_valid, n_valid_tiles, n_rows = _route_metadata(idx[:, :TOP_K], n_exp, tm_moe)
    cw_sorted = jnp.zeros((n_rows, 1), F32).at[pos, 0].set(tw[:, :TOP_K].reshape(-1))
    tg = 256
    x_sorted = _gather_tokens(n_valid_tiles * (tm_moe // tg), src, h2, n_rows, tg)
    act = _moe_up(n_valid_tiles, tile_expert, tile_valid, x_sorted, w_exp_gate[0], b_exp_gate[0],
                  w_exp_up[0], b_exp_up[0], tm_moe, 512, sub)
    y_sorted = _moe_down(n_valid_tiles, tile_expert, tile_valid, act, w_exp_down[0], b_exp_down[0], cw_sorted,
                         tm_moe, 1024, sub)
    y_p, y_s = _combine(pos, y_sorted, x1_p, x1_s, mod_p[5], mod_s[5], ln_final_g.reshape(1, d), 128)

    def rows5(a, b_, t_):
        return a.reshape(depth, b_, t_, n_heads, HEAD_DIM)

    return (y_p.reshape(bp, seq, d), y_s.reshape(bs, dseq, d),
            rows5(sb_p[1], bp, seq), rows5(sb_p[2], bp, seq), rows5(fx_p[1], bp, seq), rows5(fx_p[2], bp, seq),
            lf_p.reshape(depth, bp, seq, n_heads),
            rows5(sb_s[1], bs, dseq), rows5(sb_s[2], bs, dseq), rows5(fx_s[1], bs, dseq), rows5(fx_s[2], bs, dseq),
            lf_s.reshape(depth, bs, dseq, n_heads))
```

```python
import functools

import jax
import jax.numpy as jnp
from jax import lax
from jax.experimental import pallas as pl
from jax.experimental.pallas import tpu as pltpu

F32 = jnp.float32
BF16 = jnp.bfloat16

HEAD_DIM = 128
TOP_K = 4
N_MOD = 6
EPS = 1e-6
SWIGLU_ALPHA = 1.702
SWIGLU_LIMIT = 7.0
NEG_BIG = -1e30
LOG2E = 1.4426950408889634

V7X_LANES = 128
V7X_MXU_DIM = 256
V7X_VMEM_BUDGET_BYTES = 56 * 1024 * 1024


def _vmem_limit(block_bytes, scratch_bytes=0, temp_bytes=0):
    need = 2 * sum(block_bytes) + scratch_bytes + temp_bytes + (2 << 20)
    return int(min(max(need, 16 << 20), V7X_VMEM_BUDGET_BYTES))


def _params(semantics, vmem):
    return pltpu.CompilerParams(dimension_semantics=semantics, vmem_limit_bytes=vmem)


def _nbytes(shape, dtype):
    n = 1
    for s in shape:
        n *= s
    return n * jnp.dtype(dtype).itemsize


def _log_sigmoid(x):
    return jnp.minimum(x, 0.0) - jnp.log1p(jnp.exp(-jnp.abs(x)))


def _split3(x):
    hi = x.astype(BF16)
    r1 = x - hi.astype(F32)
    mid = r1.astype(BF16)
    lo = (r1 - mid.astype(F32)).astype(BF16)
    return hi, mid, lo


def _ada_kernel(c_ref, w_ref, b_ref, o_ref):
    c = c_ref[...]
    a = (c * jax.nn.sigmoid(c)).astype(BF16)
    o_ref[...] = jnp.dot(a, w_ref[...].astype(BF16), preferred_element_type=F32) + b_ref[...]


def _ada_mod(c_all, w_ada, b_ada):
    m, d = c_all.shape
    n = w_ada.shape[1]
    tn = 512
    return pl.pallas_call(
        _ada_kernel,
        grid=(n // tn,),
        in_specs=[pl.BlockSpec((m, d), lambda j: (0, 0)),
                  pl.BlockSpec((d, tn), lambda j: (0, j)),
                  pl.BlockSpec((1, tn), lambda j: (0, j))],
        out_specs=pl.BlockSpec((m, tn), lambda j: (0, j)),
        out_shape=jax.ShapeDtypeStruct((m, n), F32),
        compiler_params=_params(("arbitrary",), _vmem_limit(
            [_nbytes((d, tn), F32)], temp_bytes=_nbytes((d, tn), BF16))),
        name="ada_mod",
    )(c_all, w_ada, b_ada.reshape(1, n))


def _norm_mod(x, g, sc, sh):
    ms = jnp.mean(x * x, axis=-1, keepdims=True)
    h = x * lax.rsqrt(ms + EPS) * g
    return h * (1.0 + sc) + sh


def _norm1_kernel(x_ref, g_ref, sc_ref, sh_ref, wf_ref, bf_ref, h_ref, lf_ref):
    h = _norm_mod(x_ref[...], g_ref[...], sc_ref[...], sh_ref[...])
    hb = h.astype(BF16)
    h_ref[...] = hb
    f = jnp.dot(hb, wf_ref[...], preferred_element_type=F32)
    nh = lf_ref.shape[1]
    lf_ref[...] = _log_sigmoid(f[:, :nh] + bf_ref[...])


def _norm1(x, g, sc, sh, wf_pad, b_forget, tm):
    m, d = x.shape
    nh = b_forget.shape[1]
    rows = sc.shape[0]
    mod_spec = pl.BlockSpec((1, d), lambda i: (0, 0)) if rows == 1 else pl.BlockSpec((tm, d), lambda i: (i, 0))
    return pl.pallas_call(
        _norm1_kernel,
        grid=(m // tm,),
        in_specs=[pl.BlockSpec((tm, d), lambda i: (i, 0)),
                  pl.BlockSpec((1, d), lambda i: (0, 0)),
                  mod_spec, mod_spec,
                  pl.BlockSpec((d, V7X_LANES), lambda i: (0, 0)),
                  pl.BlockSpec((1, nh), lambda i: (0, 0))],
        out_specs=[pl.BlockSpec((tm, d), lambda i: (i, 0)),
                   pl.BlockSpec((tm, nh), lambda i: (i, 0))],
        out_shape=[jax.ShapeDtypeStruct((m, d), BF16), jax.ShapeDtypeStruct((m, nh), F32)],
        compiler_params=_params(("parallel",), _vmem_limit(
            [_nbytes((tm, d), F32) * (1 if rows == 1 else 3), _nbytes((tm, d), BF16)],
            temp_bytes=2 * _nbytes((tm, d), F32))),
        name="norm1",
    )(x, g, sc, sh, wf_pad, b_forget)


def _norm2_kernel(x_ref, g_ref, sc_ref, sh_ref, wrh_ref, wrl_ref, br_ref, *rest, n_real):
    h_ref, idx_ref, tw_ref = rest[-3:]
    i = pl.program_id(0)

    @pl.when(i < n_real)
    def _():
        _norm2_tile(x_ref, g_ref, sc_ref, sh_ref, wrh_ref, wrl_ref, br_ref, h_ref, idx_ref, tw_ref)

    @pl.when(i >= n_real)
    def _():
        h_ref[...] = jnp.zeros(h_ref.shape, F32)
        idx_ref[...] = jnp.zeros(idx_ref.shape, jnp.int32)
        tw_ref[...] = jnp.zeros(tw_ref.shape, F32)


def _norm2_tile(x_ref, g_ref, sc_ref, sh_ref, wrh_ref, wrl_ref, br_ref, h_ref, idx_ref, tw_ref):
    h = _norm_mod(x_ref[...], g_ref[...], sc_ref[...], sh_ref[...])
    h_ref[...] = h
    hb = h.astype(BF16)
    hl = (h - hb.astype(F32)).astype(BF16)
    wrh = wrh_ref[...]
    logits = (jnp.dot(hb, wrh, preferred_element_type=F32)
              + jnp.dot(hb, wrl_ref[...], preferred_element_type=F32)
              + jnp.dot(hl, wrh, preferred_element_type=F32)) + br_ref[...]
    lane = lax.broadcasted_iota(jnp.int32, logits.shape, 1)
    cur = logits
    vals, idxs = [], []
    for _ in range(TOP_K):
        mx = jnp.max(cur, axis=1, keepdims=True)
        ix = jnp.min(jnp.where(cur == mx, lane, V7X_LANES), axis=1, keepdims=True)
        vals.append(mx)
        idxs.append(ix)
        cur = jnp.where(lane == ix, -jnp.inf, cur)
    exps = [jnp.exp(v - vals[0]) for v in vals]
    den = exps[0]
    for e in exps[1:]:
        den = den + e
    idx_out = jnp.zeros(logits.shape, jnp.int32)
    w_out = jnp.zeros(logits.shape, F32)
    for k in range(TOP_K):
        idx_out = jnp.where(lane == k, idxs[k], idx_out)
        w_out = jnp.where(lane == k, exps[k] / den, w_out)
    idx_ref[...] = idx_out
    tw_ref[...] = w_out


def _norm2_router(x, g, sc, sh, wr_hi, wr_lo, br_pad, tm, n_total, row_off, prev=None):
    m, d = x.shape
    rows = sc.shape[0]
    assert row_off % tm == 0 and (n_total - row_off - m) % tm == 0
    off = row_off // tm
    n_real = m // tm
    n_zero = (n_total - row_off - m) // tm if prev is None else 0
    real = lambda i: jnp.minimum(i, n_real - 1)
    mod_spec = pl.BlockSpec((1, d), lambda i: (0, 0)) if rows == 1 else pl.BlockSpec((tm, d), lambda i: (real(i), 0))
    in_specs = [pl.BlockSpec((tm, d), lambda i: (real(i), 0)),
                pl.BlockSpec((1, d), lambda i: (0, 0)),
                mod_spec, mod_spec,
                pl.BlockSpec((d, V7X_LANES), lambda i: (0, 0)),
                pl.BlockSpec((d, V7X_LANES), lambda i: (0, 0)),
                pl.BlockSpec((1, V7X_LANES), lambda i: (0, 0))]
    args = [x, g, sc, sh, wr_hi, wr_lo, br_pad]
    aliases = {}
    if prev is not None:
        in_specs += [pl.BlockSpec(memory_space=pl.ANY)] * 3
        aliases = {len(args) + k: k for k in range(3)}
        args += list(prev)
    return pl.pallas_call(
        functools.partial(_norm2_kernel, n_real=n_real),
        grid=(n_real + n_zero,),
        in_specs=in_specs,
        out_specs=[pl.BlockSpec((tm, d), lambda i: (i + off, 0)),
                   pl.BlockSpec((tm, V7X_LANES), lambda i: (i + off, 0)),
                   pl.BlockSpec((tm, V7X_LANES), lambda i: (i + off, 0))],
        out_shape=[jax.ShapeDtypeStruct((n_total, d), F32),
                   jax.ShapeDtypeStruct((n_total, V7X_LANES), jnp.int32),
                   jax.ShapeDtypeStruct((n_total, V7X_LANES), F32)],
        input_output_aliases=aliases,
        compiler_params=_params(("parallel",), _vmem_limit(
            [_nbytes((tm, d), F32) * (2 if rows == 1 else 4)],
            temp_bytes=3 * _nbytes((tm, d), F32))),
        name="norm2_router",
    )(*args)


def _proj_kernel(a_ref, w_ref, *o_refs, scale, want_f32, want_bf16):
    acc = jnp.dot(a_ref[...], w_ref[...], preferred_element_type=F32)
    k = 0
    if want_f32:
        o_refs[k][...] = acc
        k += 1
    if want_bf16:
        o_refs[k][...] = (acc * scale).astype(BF16)


def _proj(a, w, col0, n, tm, tn, *, scale=1.0, want_f32, want_bf16):
    m, kd = a.shape
    cb0 = col0 // tn
    out_specs, out_shape, ob = [], [], []
    if want_f32:
        out_specs.append(pl.BlockSpec((tm, tn), lambda i, j: (i, j)))
        out_shape.append(jax.ShapeDtypeStruct((m, n), F32))
        ob.append(_nbytes((tm, tn), F32))
    if want_bf16:
        out_specs.append(pl.BlockSpec((tm, tn), lambda i, j: (i, j)))
        out_shape.append(jax.ShapeDtypeStruct((m, n), BF16))
        ob.append(_nbytes((tm, tn), BF16))
    return pl.pallas_call(
        functools.partial(_proj_kernel, scale=scale, want_f32=want_f32, want_bf16=want_bf16),
        grid=(m // tm, n // tn),
        in_specs=[pl.BlockSpec((tm, kd), lambda i, j: (i, 0)),
                  pl.BlockSpec((kd, tn), lambda i, j: (0, cb0 + j))],
        out_specs=out_specs,
        out_shape=out_shape,
        compiler_params=_params(("parallel", "arbitrary"), _vmem_limit(
            [_nbytes((tm, kd), BF16), _nbytes((kd, tn), BF16)] + ob,
            temp_bytes=_nbytes((tm, tn), F32))),
        name="proj",
    )(a, w)


def _merge_kernel(osb_ref, ofx_ref, h_ref, wsb_ref, wfx_ref, wg1_ref, wg2_ref, bg1_ref, bg2_ref, o_ref):
    h = h_ref[...]
    bs = jnp.dot(osb_ref[...], wsb_ref[...], preferred_element_type=F32)
    bx = jnp.dot(ofx_ref[...], wfx_ref[...], preferred_element_type=F32)
    g1 = jax.nn.sigmoid(jnp.dot(h, wg1_ref[...], preferred_element_type=F32) + bg1_ref[...])
    g2 = jax.nn.sigmoid(jnp.dot(h, wg2_ref[...], preferred_element_type=F32) + bg2_ref[...])
    o_ref[...] = (g1 * bs + g2 * bx).astype(BF16)


def _merge(o_sb, o_fx, h, w_bsb, w_bfx, w_gate, b_gate, tm, tn):
    m, d = h.shape
    ws = o_sb.shape[1]
    wx = o_fx.shape[1]
    nb = d // tn
    return pl.pallas_call(
        _merge_kernel,
        grid=(m // tm, nb),
        in_specs=[pl.BlockSpec((tm, ws), lambda i, j: (i, 0)),
                  pl.BlockSpec((tm, wx), lambda i, j: (i, 0)),
                  pl.BlockSpec((tm, d), lambda i, j: (i, 0)),
                  pl.BlockSpec((ws, tn), lambda i, j: (0, j)),
                  pl.BlockSpec((wx, tn), lambda i, j: (0, j)),
                  pl.BlockSpec((d, tn), lambda i, j: (0, j)),
                  pl.BlockSpec((d, tn), lambda i, j: (0, nb + j)),
                  pl.BlockSpec((1, tn), lambda i, j: (0, j)),
                  pl.BlockSpec((1, tn), lambda i, j: (0, nb + j))],
        out_specs=pl.BlockSpec((tm, tn), lambda i, j: (i, j)),
        out_shape=jax.ShapeDtypeStruct((m, d), BF16),
        compiler_params=_params(("parallel", "arbitrary"), _vmem_limit(
            [_nbytes((tm, ws + wx + d), BF16), _nbytes((ws + wx + 2 * d, tn), BF16), _nbytes((tm, tn), BF16)],
            temp_bytes=6 * _nbytes((tm, tn), F32))),
        name="merge",
    )(o_sb, o_fx, h, w_bsb, w_bfx, w_gate, w_gate, b_gate, b_gate)


def _outproj_kernel(a_ref, w_ref, x_ref, g_ref, o_ref):
    acc = jnp.dot(a_ref[...], w_ref[...], preferred_element_type=F32)
    o_ref[...] = x_ref[...] + g_ref[...] * acc


def _outproj(a, w, x, gate, tm, tn):
    m, kd = a.shape
    n = w.shape[1]
    rows = gate.shape[0]
    g_spec = pl.BlockSpec((1, tn), lambda i, j: (0, j)) if rows == 1 else pl.BlockSpec((tm, tn), lambda i, j: (i, j))
    return pl.pallas_call(
        _outproj_kernel,
        grid=(m // tm, n // tn),
        in_specs=[pl.BlockSpec((tm, kd), lambda i, j: (i, 0)),
                  pl.BlockSpec((kd, tn), lambda i, j: (0, j)),
                  pl.BlockSpec((tm, tn), lambda i, j: (i, j)),
                  g_spec],
        out_specs=pl.BlockSpec((tm, tn), lambda i, j: (i, j)),
        out_shape=jax.ShapeDtypeStruct((m, n), F32),
        compiler_params=_params(("parallel", "arbitrary"), _vmem_limit(
            [_nbytes((tm, kd), BF16), _nbytes((kd, tn), BF16), 3 * _nbytes((tm, tn), F32)],
            temp_bytes=_nbytes((tm, tn), F32))),
        name="outproj",
    )(a, w, x, gate)


def _cumsum_kernel(lf_ref, tri_ref, o_ref, *, chunk):
    t = lf_ref.shape[1]
    nh = lf_ref.shape[2]
    carry = jnp.zeros((1, nh), F32)
    start = 0
    while start < t:
        c = min(chunk, t - start)
        x = lf_ref[0, start:start + c, :]
        tri = tri_ref[:c, :c]
        hi, mid, lo = _split3(x)
        s = (jnp.dot(tri, hi, preferred_element_type=F32)
             + jnp.dot(tri, mid, preferred_element_type=F32)
             + jnp.dot(tri, lo, preferred_element_type=F32)) + carry
        o_ref[0, start:start + c, :] = s
        carry = s[c - 1:c, :]
        start += c


def _cumsum_time(lf, tri_incl, chunk):
    b, t, nh = lf.shape
    return pl.pallas_call(
        functools.partial(_cumsum_kernel, chunk=chunk),
        grid=(b,),
        in_specs=[pl.BlockSpec((1, t, nh), lambda i: (i, 0, 0)),
                  pl.BlockSpec((chunk, chunk), lambda i: (0, 0))],
        out_specs=pl.BlockSpec((1, t, nh), lambda i: (i, 0, 0)),
        out_shape=jax.ShapeDtypeStruct((b, t, nh), F32),
        compiler_params=_params(("parallel",), _vmem_limit(
            [2 * _nbytes((t, V7X_LANES), F32)], temp_bytes=4 << 20)),
        name="cumsum_logf",
    )(lf, tri_incl)


def _qk(q, k):
    if q.ndim == 3:
        return jnp.einsum("hqd,hkd->hqk", q, k, preferred_element_type=F32)
    return lax.dot_general(q, k, (((1,), (1,)), ((), ())), preferred_element_type=F32)


def _pv(p, v):
    if p.ndim == 3:
        return jnp.einsum("hqk,hkd->hqd", p, v, preferred_element_type=F32)
    return jnp.dot(p, v, preferred_element_type=F32)


def _rows_dot(x, m):
    if x.ndim == 3:
        h, r, n = x.shape
        return jnp.dot(x.reshape(h * r, n), m, preferred_element_type=F32).reshape(h, r, n)
    return jnp.dot(x, m, preferred_element_type=F32)


def _sb_step(q, k, v, carry, acc, tri, mask):
    hw = tri.shape[0]
    groups = k.shape[-2] // hw
    z = _qk(q, k)
    ls = -(jnp.maximum(z, 0.0) + jnp.log(1.0 + jnp.exp2(-jnp.abs(z))) * LOG2E)
    lsm = ls if mask is None else jnp.where(mask, ls, 0.0)
    hi = lsm.astype(BF16)
    lo = (lsm - hi.astype(F32)).astype(BF16)
    lw = z + ls
    parts = [None] * groups
    for g in reversed(range(groups)):
        c0 = g * hw
        cs = _rows_dot(hi[..., c0:c0 + hw], tri) + _rows_dot(lo[..., c0:c0 + hw], tri)
        parts[g] = jnp.exp2(lw[..., c0:c0 + hw] + cs + carry)
        carry = carry + cs[..., :1] + lsm[..., c0:c0 + 1]
    w = parts[0] if groups == 1 else jnp.concatenate(parts, axis=-1)
    if mask is not None:
        w = jnp.where(mask, w, 0.0)
    acc = acc + _pv(w.astype(BF16), v)
    return carry, acc


def _strict_lower_mask(tq, tk):
    row = lax.broadcasted_iota(jnp.int32, (tq, tk), 0)
    col = lax.broadcasted_iota(jnp.int32, (tq, tk), 1)
    return col < row


def _head_group(ref, rows, hp):
    return jnp.stack([ref[rows, h * HEAD_DIM:(h + 1) * HEAD_DIM] for h in range(hp)])


def _store_head_group(o_ref, x):
    for h in range(x.shape[0]):
        o_ref[:, h * HEAD_DIM:(h + 1) * HEAD_DIM] = x[h].astype(o_ref.dtype)


def _sb_prompt_kernel(q_ref, k_ref, v_ref, tri_ref, o_ref, *, tq, hp):
    i = pl.program_id(1)
    q = _head_group(q_ref, slice(None), hp)
    tri = tri_ref[...]
    d0 = pl.multiple_of(i * tq, tq)
    carry = jnp.zeros((hp, tq, 1), F32)
    acc = jnp.zeros((hp, tq, HEAD_DIM), F32)
    carry, acc = _sb_step(q, _head_group(k_ref, pl.ds(d0, tq), hp), _head_group(v_ref, pl.ds(d0, tq), hp),
                          carry, acc, tri, _strict_lower_mask(tq, tq))

    def body(n, c):
        r0 = pl.multiple_of((i - 1 - n) * tq, tq)
        return _sb_step(q, _head_group(k_ref, pl.ds(r0, tq), hp), _head_group(v_ref, pl.ds(r0, tq), hp),
                        c[0], c[1], tri, None)

    carry, acc = lax.fori_loop(0, i, body, (carry, acc))
    _store_head_group(o_ref, acc)


def _sb_prompt(q, k, v, tri, n_heads, tq, hp):
    t = q.shape[0]
    wb = hp * HEAD_DIM
    return pl.pallas_call(
        functools.partial(_sb_prompt_kernel, tq=tq, hp=hp),
        grid=(n_heads // hp, t // tq),
        in_specs=[pl.BlockSpec((tq, wb), lambda g, i: (i, g)),
                  pl.BlockSpec((t, wb), lambda g, i: (0, g)),
                  pl.BlockSpec((t, wb), lambda g, i: (0, g)),
                  pl.BlockSpec(tri.shape, lambda g, i: (0, 0))],
        out_specs=pl.BlockSpec((tq, wb), lambda g, i: (i, g)),
        out_shape=jax.ShapeDtypeStruct((t, n_heads * HEAD_DIM), BF16),
        compiler_params=_params(("parallel", "arbitrary"), _vmem_limit(
            [2 * _nbytes((t, wb), BF16), _nbytes(tri.shape, BF16)],
            temp_bytes=12 * hp * _nbytes((tq, tq), F32))),
        name="sb_prompt",
    )(q, k, v, tri)


def _past_heads(ref, n_heads, tk):
    return jnp.stack([ref[pl.ds(h, tk, stride=n_heads), :] for h in range(n_heads)]).astype(BF16)


def _new_heads(ref, n_heads):
    return jnp.stack([ref[0, :, h * HEAD_DIM:(h + 1) * HEAD_DIM] for h in range(n_heads)])


def _sb_sample_kernel(q_ref, kn_ref, vn_ref, kp_ref, vp_ref, tri_ref, trin_ref, o_ref, carry_s, acc_s,
                      *, n_heads, tk):
    j = pl.program_id(1)
    tq = q_ref.shape[1]
    q = _new_heads(q_ref, n_heads)

    @pl.when(j == 0)
    def _():
        c, a = _sb_step(q, _new_heads(kn_ref, n_heads), _new_heads(vn_ref, n_heads),
                        jnp.zeros((n_heads, tq, 1), F32), jnp.zeros((n_heads, tq, HEAD_DIM), F32),
                        trin_ref[...], _strict_lower_mask(tq, tq))
        carry_s[...] = jnp.broadcast_to(c, carry_s.shape)
        acc_s[...] = a

    c, a = _sb_step(q, _past_heads(kp_ref, n_heads, tk), _past_heads(vp_ref, n_heads, tk),
                    carry_s[:, :, :1], acc_s[...], tri_ref[...], None)
    carry_s[...] = jnp.broadcast_to(c, carry_s.shape)
    acc_s[...] = a

    @pl.when(j == pl.num_programs(1) - 1)
    def _():
        for h in range(n_heads):
            o_ref[0, :, h * HEAD_DIM:(h + 1) * HEAD_DIM] = acc_s[h].astype(BF16)


def _sb_sample(q, k_new, v_new, k_past, v_past, tri, tri_new, n_heads, tk):
    b, tq, width = q.shape
    p = k_past.shape[0] // (b * n_heads)
    nc = p // tk
    new_spec = pl.BlockSpec((1, tq, width), lambda bi, j: (bi, 0, 0))
    past_spec = pl.BlockSpec((tk * n_heads, HEAD_DIM), lambda bi, j: (bi * nc + (nc - 1 - j), 0))
    return pl.pallas_call(
        functools.partial(_sb_sample_kernel, n_heads=n_heads, tk=tk),
        grid=(b, nc),
        in_specs=[new_spec, new_spec, new_spec, past_spec, past_spec,
                  pl.BlockSpec(tri.shape, lambda bi, j: (0, 0)),
                  pl.BlockSpec(tri_new.shape, lambda bi, j: (0, 0))],
        out_specs=new_spec,
        out_shape=jax.ShapeDtypeStruct((b, tq, width), BF16),
        scratch_shapes=[pltpu.VMEM((n_heads, tq, V7X_LANES), F32), pltpu.VMEM((n_heads, tq, HEAD_DIM), F32)],
        compiler_params=_params(("parallel", "arbitrary"), _vmem_limit(
            [2 * _nbytes((tk * n_heads, HEAD_DIM), F32), 4 * _nbytes((tq, width), BF16)],
            scratch_bytes=2 * _nbytes((n_heads, tq, V7X_LANES), F32), temp_bytes=8 << 20)),
        name="sb_sample",
    )(q, k_new, v_new, k_past, v_past, tri, tri_new)


def _fox_step(q, k, v, fk_row, fq_col, m, l, acc, mask):
    s = _qk(q, k) - fk_row
    if mask is not None:
        s = jnp.where(mask, s, NEG_BIG)
    m_new = jnp.maximum(m, jnp.max(s, axis=-1, keepdims=True) + fq_col)
    pexp = jnp.exp2(s + (fq_col - m_new))
    alpha = jnp.exp2(m - m_new)
    l = alpha * l + jnp.sum(pexp, axis=-1, keepdims=True)
    acc = alpha * acc + _pv(pexp.astype(BF16), v)
    return m_new, l, acc


def _lower_incl_mask(tq, tk):
    row = lax.broadcasted_iota(jnp.int32, (tq, tk), 0)
    col = lax.broadcasted_iota(jnp.int32, (tq, tk), 1)
    return col <= row


def _pick_head_column(f_cols, h):
    lane = lax.broadcasted_iota(jnp.int32, f_cols.shape, 1)
    return jnp.sum(jnp.where(lane == h, f_cols, 0.0), axis=1, keepdims=True)


def _fox_prompt_kernel(q_ref, k_ref, v_ref, fr_ref, fc_ref, o_ref, *, tq):
    h = pl.program_id(0)
    i = pl.program_id(1)
    q = q_ref[...]
    fq = _pick_head_column(fc_ref[...], h) * LOG2E
    d0 = pl.multiple_of(i * tq, tq)
    m = jnp.full((tq, 1), NEG_BIG, F32)
    l = jnp.zeros((tq, 1), F32)
    acc = jnp.zeros((tq, HEAD_DIM), F32)
    m, l, acc = _fox_step(q, k_ref[pl.ds(d0, tq), :], v_ref[pl.ds(d0, tq), :],
                          fr_ref[0, :, pl.ds(d0, tq)] * LOG2E, fq, m, l, acc, _lower_incl_mask(tq, tq))

    def keys(r0, w, c):
        return _fox_step(q, k_ref[pl.ds(r0, w), :], v_ref[pl.ds(r0, w), :],
                         fr_ref[0, :, pl.ds(r0, w)] * LOG2E, fq, c[0], c[1], c[2], None)

    state = lax.fori_loop(0, i // 2, lambda n, c: keys(pl.multiple_of(n * 2 * tq, 2 * tq), 2 * tq, c), (m, l, acc))
    m, l, acc = lax.cond(i % 2 == 1, lambda c: keys(pl.multiple_of((i - 1) * tq, tq), tq, c), lambda c: c, state)
    o_ref[...] = (acc / l).astype(BF16)


def _fox_prompt(q, k, v, f_rows, f_cols, n_heads, tq):
    t = q.shape[0]
    return pl.pallas_call(
        functools.partial(_fox_prompt_kernel, tq=tq),
        grid=(n_heads, t // tq),
        in_specs=[pl.BlockSpec((tq, HEAD_DIM), lambda h, i: (i, h)),
                  pl.BlockSpec((t, HEAD_DIM), lambda h, i: (0, h)),
                  pl.BlockSpec((t, HEAD_DIM), lambda h, i: (0, h)),
                  pl.BlockSpec((1, 1, t), lambda h, i: (h, 0, 0)),
                  pl.BlockSpec((tq, n_heads), lambda h, i: (i, 0))],
        out_specs=pl.BlockSpec((tq, HEAD_DIM), lambda h, i: (i, h)),
        out_shape=jax.ShapeDtypeStruct((t, n_heads * HEAD_DIM), BF16),
        compiler_params=_params(("parallel", "arbitrary"), _vmem_limit(
            [2 * _nbytes((t, HEAD_DIM), BF16), _nbytes((8, t), F32)],
            temp_bytes=10 * _nbytes((tq, 2 * tq), F32))),
        name="fox_prompt",
    )(q, k, v, f_rows, f_cols)


def _fox_sample_kernel(q_ref, kn_ref, vn_ref, kp_ref, vp_ref, frp_ref, frn_ref, fc_ref, o_ref, m_s, l_s, acc_s,
                       *, n_heads, tk):
    j = pl.program_id(1)
    tq = q_ref.shape[1]
    q = _new_heads(q_ref, n_heads)
    fc = fc_ref[0] * LOG2E
    fq = jnp.stack([fc[:, h:h + 1] for h in range(n_heads)])

    @pl.when(j == 0)
    def _():
        m, l, a = _fox_step(q, _new_heads(kn_ref, n_heads), _new_heads(vn_ref, n_heads),
                            (frn_ref[0] * LOG2E)[:, None, :], fq,
                            jnp.full((n_heads, tq, 1), NEG_BIG, F32), jnp.zeros((n_heads, tq, 1), F32),
                            jnp.zeros((n_heads, tq, HEAD_DIM), F32), _lower_incl_mask(tq, tq))
        m_s[...] = jnp.broadcast_to(m, m_s.shape)
        l_s[...] = jnp.broadcast_to(l, l_s.shape)
        acc_s[...] = a

    m, l, a = _fox_step(q, _past_heads(kp_ref, n_heads, tk), _past_heads(vp_ref, n_heads, tk),
                        (frp_ref[0] * LOG2E)[:, None, :], fq,
                        m_s[:, :, :1], l_s[:, :, :1], acc_s[...], None)
    m_s[...] = jnp.broadcast_to(m, m_s.shape)
    l_s[...] = jnp.broadcast_to(l, l_s.shape)
    acc_s[...] = a

    @pl.when(j == pl.num_programs(1) - 1)
    def _():
        for h in range(n_heads):
            o_ref[0, :, h * HEAD_DIM:(h + 1) * HEAD_DIM] = (acc_s[h] / l_s[h][:, :1]).astype(BF16)


def _fox_sample(q, k_new, v_new, k_past, v_past, f_rows_past, f_rows_new, f_cols_new, n_heads, tk):
    b, tq, width = q.shape
    p = f_rows_past.shape[2]
    nc = p // tk
    new_spec = pl.BlockSpec((1, tq, width), lambda bi, j: (bi, 0, 0))
    past_spec = pl.BlockSpec((tk * n_heads, HEAD_DIM), lambda bi, j: (bi * nc + j, 0))
    return pl.pallas_call(
        functools.partial(_fox_sample_kernel, n_heads=n_heads, tk=tk),
        grid=(b, nc),
        in_specs=[new_spec, new_spec, new_spec, past_spec, past_spec,
                  pl.BlockSpec((1, n_heads, tk), lambda bi, j: (bi, 0, j)),
                  pl.BlockSpec((1, n_heads, tq), lambda bi, j: (bi, 0, 0)),
                  pl.BlockSpec((1, tq, n_heads), lambda bi, j: (bi, 0, 0))],
        out_specs=new_spec,
        out_shape=jax.ShapeDtypeStruct((b, tq, width), BF16),
        scratch_shapes=[pltpu.VMEM((n_heads, tq, V7X_LANES), F32), pltpu.VMEM((n_heads, tq, V7X_LANES), F32),
                        pltpu.VMEM((n_heads, tq, HEAD_DIM), F32)],
        compiler_params=_params(("parallel", "arbitrary"), _vmem_limit(
            [2 * _nbytes((tk * n_heads, HEAD_DIM), F32), 4 * _nbytes((tq, width), BF16)],
            scratch_bytes=3 * _nbytes((n_heads, tq, V7X_LANES), F32), temp_bytes=8 << 20)),
        name="fox_sample",
    )(q, k_new, v_new, k_past, v_past, f_rows_past, f_rows_new, f_cols_new)


def _row_copy(src_hbm, dst, row_src, row_dst, sem):
    return pltpu.make_async_copy(src_hbm.at[pl.ds(row_src, 1)], dst.at[pl.ds(row_dst, 1)], sem)


def _gather_kernel(nv_ref, cur_ref, nxt_ref, x_hbm, o_ref, buf, sem, *, tg):
    i = pl.program_id(0)
    nv = nv_ref[0]

    def issue(idx_ref, slot):
        def body(r, _):
            _row_copy(x_hbm, buf.at[slot], idx_ref[r], r, sem.at[slot]).start()
            return 0

        lax.fori_loop(0, tg, body, 0, unroll=8)

    @pl.when(i == 0)
    def _():
        issue(cur_ref, 0)

    @pl.when(i + 1 < nv)
    def _():
        issue(nxt_ref, (i + 1) % 2)

    @pl.when(i < nv)
    def _():
        slot = i % 2
        pltpu.make_async_copy(x_hbm.at[pl.ds(0, tg)], buf.at[slot], sem.at[slot]).wait()
        o_ref[...] = buf[slot].astype(BF16)

    @pl.when(i >= nv)
    def _():
        o_ref[...] = jnp.zeros(o_ref.shape, BF16)


def _gather_tokens(n_valid_tiles, src, x, n_rows, tg):
    d = x.shape[1]
    nt = n_rows // tg
    return pl.pallas_call(
        functools.partial(_gather_kernel, tg=tg),
        grid_spec=pltpu.PrefetchScalarGridSpec(
            num_scalar_prefetch=1,
            grid=(nt,),
            in_specs=[pl.BlockSpec((tg,), lambda i, nv: (jnp.minimum(i, nv[0] - 1),), memory_space=pltpu.SMEM),
                      pl.BlockSpec((tg,), lambda i, nv: (jnp.minimum(i + 1, nv[0] - 1),), memory_space=pltpu.SMEM),
                      pl.BlockSpec(memory_space=pl.ANY)],
            out_specs=pl.BlockSpec((tg, d), lambda i, nv: (i, 0)),
            scratch_shapes=[pltpu.VMEM((2, tg, d), F32), pltpu.SemaphoreType.DMA((2,))]),
        out_shape=jax.ShapeDtypeStruct((n_rows, d), BF16),
        compiler_params=_params(("arbitrary",), _vmem_limit(
            [_nbytes((tg, d), BF16)], scratch_bytes=_nbytes((2, tg, d), F32), temp_bytes=_nbytes((tg, d), F32))),
        name="gather_tokens",
    )(n_valid_tiles, src, src, x)


def _tile_meta(nv_ref, te_ref, tv_ref):
    t = pl.program_id(1)
    active = t < nv_ref[0]
    prev = te_ref[jnp.maximum(t - 1, 0)]
    first = jnp.logical_and(active, jnp.logical_or(t == 0, te_ref[t] != prev))
    return active, first, tv_ref[t]


def _moe_up_kernel(nv_ref, te_ref, tv_ref, x_ref, wg_ref, wu_ref, bg_ref, bu_ref, a_ref, wg_bf, wu_bf, *, sub):
    active, first, valid = _tile_meta(nv_ref, te_ref, tv_ref)

    @pl.when(first)
    def _():
        wg_bf[...] = wg_ref[0].astype(BF16)
        wu_bf[...] = wu_ref[0].astype(BF16)

    tm = x_ref.shape[0]
    for s in range(tm // sub):
        rows = pl.ds(s * sub, sub)
        has_rows = (s + 1) * sub > tm - valid

        @pl.when(jnp.logical_and(active, has_rows))
        def _():
            x = x_ref[rows, :]
            gate = jnp.minimum(jnp.dot(x, wg_bf[...], preferred_element_type=F32) + bg_ref[0], SWIGLU_LIMIT)
            up = jnp.clip(jnp.dot(x, wu_bf[...], preferred_element_type=F32) + bu_ref[0],
                          -SWIGLU_LIMIT, SWIGLU_LIMIT)
            a_ref[rows, :] = (gate * jax.nn.sigmoid(SWIGLU_ALPHA * gate) * (up + 1.0)).astype(BF16)

        @pl.when(jnp.logical_not(jnp.logical_and(active, has_rows)))
        def _():
            a_ref[rows, :] = jnp.zeros((sub, a_ref.shape[1]), BF16)


def _clamped_tile(t, nv):
    return jnp.minimum(t, nv[0] - 1)


def _moe_up(n_valid_tiles, tile_expert, tile_valid, x_sorted, w_gate, b_gate, w_up, b_up, tm, tn, sub):
    r, d = x_sorted.shape
    n_exp, _, f = w_gate.shape
    w_spec = pl.BlockSpec((1, d, tn), lambda n, t, nv, te, tv: (te[_clamped_tile(t, nv)], 0, n))
    b_spec = pl.BlockSpec((1, 1, tn), lambda n, t, nv, te, tv: (te[_clamped_tile(t, nv)], 0, n))
    return pl.pallas_call(
        functools.partial(_moe_up_kernel, sub=sub),
        grid_spec=pltpu.PrefetchScalarGridSpec(
            num_scalar_prefetch=3,
            grid=(f // tn, r // tm),
            in_specs=[pl.BlockSpec((tm, d), lambda n, t, nv, te, tv: (_clamped_tile(t, nv), 0)),
                      w_spec, w_spec, b_spec, b_spec],
            out_specs=pl.BlockSpec((tm, tn), lambda n, t, nv, te, tv: (t, n)),
            scratch_shapes=[pltpu.VMEM((d, tn), BF16), pltpu.VMEM((d, tn), BF16)]),
        out_shape=jax.ShapeDtypeStruct((r, f), BF16),
        compiler_params=_params(("arbitrary", "arbitrary"), _vmem_limit(
            [_nbytes((tm, d), BF16), 2 * _nbytes((d, tn), F32), _nbytes((tm, tn), BF16)],
            scratch_bytes=2 * _nbytes((d, tn), BF16), temp_bytes=6 * _nbytes((sub, tn), F32))),
        name="moe_up",
    )(n_valid_tiles, tile_expert, tile_valid, x_sorted, w_gate, w_up,
      b_gate.reshape(n_exp, 1, f), b_up.reshape(n_exp, 1, f))


def _moe_down_kernel(nv_ref, te_ref, tv_ref, a_ref, wd_ref, bd_ref, cw_ref, y_ref, wd_bf, *, sub):
    active, first, valid = _tile_meta(nv_ref, te_ref, tv_ref)

    @pl.when(first)
    def _():
        wd_bf[...] = wd_ref[0].astype(BF16)

    tm = a_ref.shape[0]
    for s in range(tm // sub):
        rows = pl.ds(s * sub, sub)
        has_rows = (s + 1) * sub > tm - valid

        @pl.when(jnp.logical_and(active, has_rows))
        def _():
            y = jnp.dot(a_ref[rows, :], wd_bf[...], preferred_element_type=F32) + bd_ref[0]
            y_ref[rows, :] = cw_ref[rows, :] * y

        @pl.when(jnp.logical_not(jnp.logical_and(active, has_rows)))
        def _():
            y_ref[rows, :] = jnp.zeros((sub, y_ref.shape[1]), F32)


def _moe_down(n_valid_tiles, tile_expert, tile_valid, act, w_down, b_down, cw_sorted, tm, tn, sub):
    r, f = act.shape
    n_exp, _, d = w_down.shape
    return pl.pallas_call(
        functools.partial(_moe_down_kernel, sub=sub),
        grid_spec=pltpu.PrefetchScalarGridSpec(
            num_scalar_prefetch=3,
            grid=(d // tn, r // tm),
            in_specs=[pl.BlockSpec((tm, f), lambda n, t, nv, te, tv: (_clamped_tile(t, nv), 0)),
                      pl.BlockSpec((1, f, tn), lambda n, t, nv, te, tv: (te[_clamped_tile(t, nv)], 0, n)),
                      pl.BlockSpec((1, 1, tn), lambda n, t, nv, te, tv: (te[_clamped_tile(t, nv)], 0, n)),
                      pl.BlockSpec((tm, 1), lambda n, t, nv, te, tv: (_clamped_tile(t, nv), 0))],
            out_specs=pl.BlockSpec((tm, tn), lambda n, t, nv, te, tv: (t, n)),
            scratch_shapes=[pltpu.VMEM((f, tn), BF16)]),
        out_shape=jax.ShapeDtypeStruct((r, d), F32),
        compiler_params=_params(("arbitrary", "arbitrary"), _vmem_limit(
            [_nbytes((tm, f), BF16), _nbytes((f, tn), F32), _nbytes((tm, tn), F32),
             _nbytes((tm, V7X_LANES), F32)],
            scratch_bytes=_nbytes((f, tn), BF16), temp_bytes=3 * _nbytes((sub, tn), F32))),
        name="moe_down",
    )(n_valid_tiles, tile_expert, tile_valid, act, w_down, b_down.reshape(n_exp, 1, d), cw_sorted)


def _combine_kernel(cur_ref, nxt_ref, y_hbm, xp_ref, xs_ref, gp_ref, gs_ref, lnf_ref, op_ref, os_ref, buf, sem,
                    *, n_tiles_p, tc):
    i = pl.program_id(0)
    n = pl.num_programs(0)

    def issue(idx_ref, slot):
        def body(r, _):
            for k in range(TOP_K):
                _row_copy(y_hbm, buf.at[slot, k], idx_ref[r * TOP_K + k], r, sem.at[slot]).start()
            return 0

        lax.fori_loop(0, tc, body, 0, unroll=4)

    @pl.when(i == 0)
    def _():
        issue(cur_ref, 0)

    @pl.when(i + 1 < n)
    def _():
        issue(nxt_ref, (i + 1) % 2)

    slot = i % 2
    for k in range(TOP_K):
        pltpu.make_async_copy(y_hbm.at[pl.ds(0, tc)], buf.at[slot, k], sem.at[slot]).wait()
    ffn = (buf[slot, 0] + buf[slot, 1]) + (buf[slot, 2] + buf[slot, 3])

    def finish(x, g):
        x2 = x + g * ffn
        ms = jnp.mean(x2 * x2, axis=-1, keepdims=True)
        return x2 * lax.rsqrt(ms + EPS) * lnf_ref[...]

    @pl.when(i < n_tiles_p)
    def _():
        op_ref[...] = finish(xp_ref[...], gp_ref[...])

    @pl.when(i >= n_tiles_p)
    def _():
        os_ref[...] = finish(xs_ref[...], gs_ref[...])


def _combine(pos_flat, y_sorted, x1_p, x1_s, gm_p, gm_s, ln_final_g, tc):
    n_p, d = x1_p.shape
    n_s = x1_s.shape[0]
    ntp = n_p // tc
    nts = n_s // tc
    nt = ntp + nts
    p_map = lambda i: (jnp.minimum(i, ntp - 1), 0)
    s_map = lambda i: (jnp.maximum(i - ntp, 0), 0)
    tp = tc * TOP_K
    return pl.pallas_call(
        functools.partial(_combine_kernel, n_tiles_p=ntp, tc=tc),
        grid=(nt,),
        in_specs=[pl.BlockSpec((tp,), lambda i: (i,), memory_space=pltpu.SMEM),
                  pl.BlockSpec((tp,), lambda i: (jnp.minimum(i + 1, nt - 1),), memory_space=pltpu.SMEM),
                  pl.BlockSpec(memory_space=pl.ANY),
                  pl.BlockSpec((tc, d), p_map),
                  pl.BlockSpec((tc, d), s_map),
                  pl.BlockSpec((1, d), lambda i: (0, 0)),
                  pl.BlockSpec((tc, d), s_map),
                  pl.BlockSpec((1, d), lambda i: (0, 0))],
        out_specs=[pl.BlockSpec((tc, d), p_map), pl.BlockSpec((tc, d), s_map)],
        scratch_shapes=[pltpu.VMEM((2, TOP_K, tc, d), F32), pltpu.SemaphoreType.DMA((2,))],
        out_shape=[jax.ShapeDtypeStruct((n_p, d), F32), jax.ShapeDtypeStruct((n_s, d), F32)],
        compiler_params=_params(("arbitrary",), _vmem_limit(
            [5 * _nbytes((tc, d), F32)], scratch_bytes=_nbytes((2, TOP_K, tc, d), F32),
            temp_bytes=3 * _nbytes((tc, d), F32))),
        name="combine_final",
    )(pos_flat, pos_flat, y_sorted, x1_p, x1_s, gm_p, gm_s, ln_final_g)


def _route_metadata(top_idx, top_w, n_exp, tm):
    n_tok, k = top_idx.shape
    n_pairs = n_tok * k
    e_flat = top_idx.reshape(n_pairs)
    onehot = (e_flat[:, None] == jnp.arange(n_exp, dtype=jnp.int32)[None, :]).astype(jnp.int32)
    counts = jnp.sum(onehot, axis=0)
    rank = jnp.sum((jnp.cumsum(onehot, axis=0) - 1) * onehot, axis=1)
    padded = ((counts + tm - 1) // tm) * tm
    ends = jnp.cumsum(padded)
    first_row = ends - counts
    pos = (first_row[e_flat] + rank).astype(jnp.int32)
    n_tiles = n_pairs // tm + n_exp
    n_rows = n_tiles * tm
    tok = (jnp.arange(n_pairs, dtype=jnp.int32) // k)
    pair = jnp.stack([tok, lax.bitcast_convert_type(top_w.reshape(n_pairs), jnp.int32)], axis=1)
    placed = jnp.zeros((n_rows, 2), jnp.int32).at[pos].set(pair)
    src = placed[:, 0]
    cw_sorted = lax.bitcast_convert_type(placed[:, 1:2], F32)
    tile_start = jnp.arange(n_tiles, dtype=jnp.int32) * tm
    tile_expert = jnp.minimum(jnp.searchsorted(ends, tile_start, side="right"), n_exp - 1).astype(jnp.int32)
    tile_valid = jnp.clip(tile_start + tm - first_row[tile_expert], 0, tm).astype(jnp.int32)
    n_valid_tiles = (ends[n_exp - 1] // tm).astype(jnp.int32).reshape(1)
    return pos, src, cw_sorted, tile_expert, tile_valid, n_valid_tiles, n_rows


def _tri_strict(n):
    j = jnp.arange(n)[:, None]
    s = jnp.arange(n)[None, :]
    return (j > s).astype(BF16)


def _tri_incl(n):
    t = jnp.arange(n)[:, None]
    s = jnp.arange(n)[None, :]
    return (s <= t).astype(BF16)


def kernel(x_prompt, x_sample, cache_sb_k, cache_sb_v, cache_fox_k, cache_fox_v, cache_fox_logf, c_prompt, c_sample, ln_attn_g, ln_moe_g, w_ada, b_ada, w_in, b_forget, w_branch_sb, w_branch_fox, w_merge_gate, b_merge_gate, w_out, w_router, b_router, w_exp_gate, b_exp_gate, w_exp_up, b_exp_up, w_exp_down, b_exp_down, ln_final_g):
    bp, seq, d = x_prompt.shape
    bs, dseq, _ = x_sample.shape
    depth = w_in.shape[0]
    assert depth == 1 and bp == 1
    n_heads = b_forget.shape[1]
    width = n_heads * HEAD_DIM
    n_exp = w_router.shape[2]
    past = cache_sb_k.shape[2]
    n_p = bp * seq
    n_s = bs * dseq
    n_tok = n_p + n_s
    q_scale = HEAD_DIM ** -0.5 * LOG2E

    c_all = jnp.concatenate([c_prompt, c_sample], axis=0)
    pad = (-c_all.shape[0]) % 8
    c_all = jnp.pad(c_all, ((0, pad), (0, 0)))
    mod = _ada_mod(c_all, w_ada[0], b_ada[0])
    mod_p = jnp.split(mod[0:1], N_MOD, axis=-1)
    mod_s = [jnp.repeat(m_, dseq, axis=0) for m_ in jnp.split(mod[1:1 + bs], N_MOD, axis=-1)]

    w_in_b = w_in[0].astype(BF16)
    wf_pad = jnp.pad(w_in_b[:, 6 * width:6 * width + n_heads], ((0, 0), (0, V7X_LANES - n_heads)))
    w_bsb = w_branch_sb[0].astype(BF16)
    w_bfx = w_branch_fox[0].astype(BF16)
    w_gate = w_merge_gate[0].astype(BF16)
    b_gate = b_merge_gate[0].reshape(1, 2 * d)
    w_o = w_out[0].astype(BF16)
    wr = jnp.pad(w_router[0], ((0, 0), (0, V7X_LANES - n_exp)))
    wr_hi = wr.astype(BF16)
    wr_lo = (wr - wr_hi.astype(F32)).astype(BF16)
    br_pad = jnp.pad(b_router[0].reshape(1, n_exp), ((0, 0), (0, V7X_LANES - n_exp)), constant_values=-jnp.inf)
    g_attn = ln_attn_g[0].reshape(1, d)
    g_moe = ln_moe_g[0].reshape(1, d)
    bf = b_forget[0].reshape(1, n_heads)

    tq = 2 * V7X_MXU_DIM
    tk_s = 2 * V7X_MXU_DIM
    hp = 2
    tri_q = _tri_strict(V7X_MXU_DIM)
    tri_n = _tri_strict(dseq)
    tri_c = _tri_incl(V7X_MXU_DIM)

    def stream(x2d, mods, tm_norm, tm_mm, tn_mm):
        sh_a, sc_a, g_a = mods[0], mods[1], mods[2]
        h, lf = _norm1(x2d, g_attn, sc_a, sh_a, wf_pad, bf, tm_norm)
        pj = functools.partial(_proj, h, w_in_b, tm=tm_mm, tn=tn_mm)
        q_sb = pj(0 * width, width, scale=q_scale, want_f32=False, want_bf16=True)[0]
        k_sb, k_sb_b = pj(1 * width, width, want_f32=True, want_bf16=True)
        v_sb, v_sb_b = pj(2 * width, width, want_f32=True, want_bf16=True)
        q_fx = pj(3 * width, width, scale=q_scale, want_f32=False, want_bf16=True)[0]
        k_fx, k_fx_b = pj(4 * width, width, want_f32=True, want_bf16=True)
        v_fx, v_fx_b = pj(5 * width, width, want_f32=True, want_bf16=True)
        return h, lf, g_a, (q_sb, k_sb, v_sb, k_sb_b, v_sb_b), (q_fx, k_fx, v_fx, k_fx_b, v_fx_b)

    xp2 = x_prompt.reshape(n_p, d)
    h_p, lf_p, ga_p, sb_p, fx_p = stream(xp2, mod_p, 256, 1024, 512)
    o_sb_p = _sb_prompt(sb_p[0], sb_p[3], sb_p[4], tri_q, n_heads, tq, hp)
    fcum_p = _cumsum_time(lf_p.reshape(1, n_p, n_heads), tri_c, V7X_MXU_DIM)[0]
    f_rows_p = fcum_p.T.reshape(n_heads, 1, n_p)
    o_fx_p = _fox_prompt(fx_p[0], fx_p[3], fx_p[4], f_rows_p, fcum_p, n_heads, tq)
    merged_p = _merge(o_sb_p, o_fx_p, h_p, w_bsb, w_bfx, w_gate, b_gate, 1024, 256)
    x1_p = _outproj(merged_p, w_o, xp2, ga_p, 1024, 512)

    xs2 = x_sample.reshape(n_s, d)
    h_s, lf_s, ga_s, sb_s, fx_s = stream(xs2, mod_s, n_s, n_s, 512)
    r3 = lambda a: a.reshape(bs, dseq, width)
    c2 = lambda a: a.reshape(bs * past * n_heads, HEAD_DIM)
    o_sb_s = _sb_sample(r3(sb_s[0]), r3(sb_s[3]), r3(sb_s[4]), c2(cache_sb_k), c2(cache_sb_v),
                        tri_q, tri_n, n_heads, tk_s)
    lf_all = jnp.concatenate([cache_fox_logf[0].astype(F32), lf_s.reshape(bs, dseq, n_heads)], axis=1)
    fcum_s = _cumsum_time(lf_all, tri_c, V7X_MXU_DIM)
    f_rows_s = jnp.swapaxes(fcum_s, 1, 2)
    o_fx_s = _fox_sample(r3(fx_s[0]), r3(fx_s[3]), r3(fx_s[4]), c2(cache_fox_k), c2(cache_fox_v),
                         f_rows_s[:, :, :past], f_rows_s[:, :, past:], fcum_s[:, past:, :], n_heads, tk_s)
    merged_s = _merge(o_sb_s.reshape(n_s, width), o_fx_s.reshape(n_s, width), h_s,
                      w_bsb, w_bfx, w_gate, b_gate, n_s, 256)
    x1_s = _outproj(merged_s, w_o, xs2, ga_s, n_s, 512)

    routed = _norm2_router(x1_p, g_moe, mod_p[4], mod_p[3], wr_hi, wr_lo, br_pad, 256, n_tok, 0)
    h2, idx, tw = _norm2_router(x1_s, g_moe, mod_s[4], mod_s[3], wr_hi, wr_lo, br_pad, n_s, n_tok, n_p, prev=routed)
    tm_moe, sub = 512, 128
    pos, src, cw_sorted, tile_expert, tile_valid, n_valid_tiles, n_rows = _route_metadata(
        idx[:, :TOP_K], tw[:, :TOP_K], n_exp, tm_moe)
    tg = 256
    x_sorted = _gather_tokens(n_valid_tiles * (tm_moe // tg), src, h2, n_rows, tg)
    act = _moe_up(n_valid_tiles, tile_expert, tile_valid, x_sorted, w_exp_gate[0], b_exp_gate[0],
                  w_exp_up[0], b_exp_up[0], tm_moe, 512, sub)
    y_sorted = _moe_down(n_valid_tiles, tile_expert, tile_valid, act, w_exp_down[0], b_exp_down[0], cw_sorted,
                         tm_moe, 1024, sub)
    y_p, y_s = _combine(pos, y_sorted, x1_p, x1_s, mod_p[5], mod_s[5], ln_final_g.reshape(1, d), 128)

    def rows5(a, b_, t_):
        return a.reshape(depth, b_, t_, n_heads, HEAD_DIM)

    return (y_p.reshape(bp, seq, d), y_s.reshape(bs, dseq, d),
            rows5(sb_p[1], bp, seq), rows5(sb_p[2], bp, seq), rows5(fx_p[1], bp, seq), rows5(fx_p[2], bp, seq),
            lf_p.reshape(depth, bp, seq, n_heads),
            rows5(sb_s[1], bs, dseq), rows5(sb_s[2], bs, dseq), rows5(fx_s[1], bs, dseq), rows5(fx_s[2], bs, dseq),
            lf_s.reshape(depth, bs, dseq, n_heads))
```

```python
import functools

import jax
import jax.numpy as jnp
from jax import lax
from jax.experimental import pallas as pl
from jax.experimental.pallas import tpu as pltpu

F32 = jnp.float32
BF16 = jnp.bfloat16

HEAD_DIM = 128
TOP_K = 4
N_MOD = 6
EPS = 1e-6
SWIGLU_ALPHA = 1.702
SWIGLU_LIMIT = 7.0
NEG_BIG = -1e30
LOG2E = 1.4426950408889634

V7X_LANES = 128
V7X_MXU_DIM = 256
V7X_VMEM_BUDGET_BYTES = 56 * 1024 * 1024


def _vmem_limit(block_bytes, scratch_bytes=0, temp_bytes=0):
    need = 2 * sum(block_bytes) + scratch_bytes + temp_bytes + (2 << 20)
    return int(min(max(need, 16 << 20), V7X_VMEM_BUDGET_BYTES))


def _params(semantics, vmem):
    return pltpu.CompilerParams(dimension_semantics=semantics, vmem_limit_bytes=vmem)


def _nbytes(shape, dtype):
    n = 1
    for s in shape:
        n *= s
    return n * jnp.dtype(dtype).itemsize


def _log_sigmoid(x):
    return jnp.minimum(x, 0.0) - jnp.log1p(jnp.exp(-jnp.abs(x)))


def _split3(x):
    hi = x.astype(BF16)
    r1 = x - hi.astype(F32)
    mid = r1.astype(BF16)
    lo = (r1 - mid.astype(F32)).astype(BF16)
    return hi, mid, lo


def _ada_kernel(c_ref, w_ref, b_ref, o_ref):
    c = c_ref[...]
    a = (c * jax.nn.sigmoid(c)).astype(BF16)
    o_ref[...] = jnp.dot(a, w_ref[...].astype(BF16), preferred_element_type=F32) + b_ref[...]


def _ada_mod(c_all, w_ada, b_ada):
    m, d = c_all.shape
    n = w_ada.shape[1]
    tn = 512
    return pl.pallas_call(
        _ada_kernel,
        grid=(n // tn,),
        in_specs=[pl.BlockSpec((m, d), lambda j: (0, 0)),
                  pl.BlockSpec((d, tn), lambda j: (0, j)),
                  pl.BlockSpec((1, tn), lambda j: (0, j))],
        out_specs=pl.BlockSpec((m, tn), lambda j: (0, j)),
        out_shape=jax.ShapeDtypeStruct((m, n), F32),
        compiler_params=_params(("arbitrary",), _vmem_limit(
            [_nbytes((d, tn), F32)], temp_bytes=_nbytes((d, tn), BF16))),
        name="ada_mod",
    )(c_all, w_ada, b_ada.reshape(1, n))


def _norm_mod(x, g, sc, sh):
    ms = jnp.mean(x * x, axis=-1, keepdims=True)
    h = x * lax.rsqrt(ms + EPS) * g
    return h * (1.0 + sc) + sh


def _norm1_kernel(x_ref, g_ref, sc_ref, sh_ref, wf_ref, bf_ref, h_ref, lf_ref):
    h = _norm_mod(x_ref[...], g_ref[...], sc_ref[...], sh_ref[...])
    hb = h.astype(BF16)
    h_ref[...] = hb
    f = jnp.dot(hb, wf_ref[...], preferred_element_type=F32)
    nh = lf_ref.shape[1]
    lf_ref[...] = _log_sigmoid(f[:, :nh] + bf_ref[...])


def _norm1(x, g, sc, sh, wf_pad, b_forget, tm):
    m, d = x.shape
    nh = b_forget.shape[1]
    rows = sc.shape[0]
    mod_spec = pl.BlockSpec((1, d), lambda i: (0, 0)) if rows == 1 else pl.BlockSpec((tm, d), lambda i: (i, 0))
    return pl.pallas_call(
        _norm1_kernel,
        grid=(m // tm,),
        in_specs=[pl.BlockSpec((tm, d), lambda i: (i, 0)),
                  pl.BlockSpec((1, d), lambda i: (0, 0)),
                  mod_spec, mod_spec,
                  pl.BlockSpec((d, V7X_LANES), lambda i: (0, 0)),
                  pl.BlockSpec((1, nh), lambda i: (0, 0))],
        out_specs=[pl.BlockSpec((tm, d), lambda i: (i, 0)),
                   pl.BlockSpec((tm, nh), lambda i: (i, 0))],
        out_shape=[jax.ShapeDtypeStruct((m, d), BF16), jax.ShapeDtypeStruct((m, nh), F32)],
        compiler_params=_params(("parallel",), _vmem_limit(
            [_nbytes((tm, d), F32) * (1 if rows == 1 else 3), _nbytes((tm, d), BF16)],
            temp_bytes=2 * _nbytes((tm, d), F32))),
        name="norm1",
    )(x, g, sc, sh, wf_pad, b_forget)


def _norm2_kernel(x_ref, g_ref, sc_ref, sh_ref, wrh_ref, wrl_ref, br_ref, *rest, n_real):
    h_ref, idx_ref, tw_ref = rest[-3:]
    i = pl.program_id(0)

    @pl.when(i < n_real)
    def _():
        _norm2_tile(x_ref, g_ref, sc_ref, sh_ref, wrh_ref, wrl_ref, br_ref, h_ref, idx_ref, tw_ref)

    @pl.when(i >= n_real)
    def _():
        h_ref[...] = jnp.zeros(h_ref.shape, F32)
        idx_ref[...] = jnp.zeros(idx_ref.shape, jnp.int32)
        tw_ref[...] = jnp.zeros(tw_ref.shape, F32)


def _norm2_tile(x_ref, g_ref, sc_ref, sh_ref, wrh_ref, wrl_ref, br_ref, h_ref, idx_ref, tw_ref):
    h = _norm_mod(x_ref[...], g_ref[...], sc_ref[...], sh_ref[...])
    h_ref[...] = h
    hb = h.astype(BF16)
    hl = (h - hb.astype(F32)).astype(BF16)
    wrh = wrh_ref[...]
    logits = (jnp.dot(hb, wrh, preferred_element_type=F32)
              + jnp.dot(hb, wrl_ref[...], preferred_element_type=F32)
              + jnp.dot(hl, wrh, preferred_element_type=F32)) + br_ref[...]
    lane = lax.broadcasted_iota(jnp.int32, logits.shape, 1)
    cur = logits
    vals, idxs = [], []
    for _ in range(TOP_K):
        mx = jnp.max(cur, axis=1, keepdims=True)
        ix = jnp.min(jnp.where(cur == mx, lane, V7X_LANES), axis=1, keepdims=True)
        vals.append(mx)
        idxs.append(ix)
        cur = jnp.where(lane == ix, -jnp.inf, cur)
    exps = [jnp.exp(v - vals[0]) for v in vals]
    den = exps[0]
    for e in exps[1:]:
        den = den + e
    idx_out = jnp.zeros(logits.shape, jnp.int32)
    w_out = jnp.zeros(logits.shape, F32)
    for k in range(TOP_K):
        idx_out = jnp.where(lane == k, idxs[k], idx_out)
        w_out = jnp.where(lane == k, exps[k] / den, w_out)
    idx_ref[...] = idx_out
    tw_ref[...] = w_out


def _norm2_router(x, g, sc, sh, wr_hi, wr_lo, br_pad, tm, n_total, row_off, prev=None):
    m, d = x.shape
    rows = sc.shape[0]
    assert row_off % tm == 0 and (n_total - row_off - m) % tm == 0
    off = row_off // tm
    n_real = m // tm
    n_zero = (n_total - row_off - m) // tm if prev is None else 0
    real = lambda i: jnp.minimum(i, n_real - 1)
    mod_spec = pl.BlockSpec((1, d), lambda i: (0, 0)) if rows == 1 else pl.BlockSpec((tm, d), lambda i: (real(i), 0))
    in_specs = [pl.BlockSpec((tm, d), lambda i: (real(i), 0)),
                pl.BlockSpec((1, d), lambda i: (0, 0)),
                mod_spec, mod_spec,
                pl.BlockSpec((d, V7X_LANES), lambda i: (0, 0)),
                pl.BlockSpec((d, V7X_LANES), lambda i: (0, 0)),
                pl.BlockSpec((1, V7X_LANES), lambda i: (0, 0))]
    args = [x, g, sc, sh, wr_hi, wr_lo, br_pad]
    aliases = {}
    if prev is not None:
        in_specs += [pl.BlockSpec(memory_space=pl.ANY)] * 3
        aliases = {len(args) + k: k for k in range(3)}
        args += list(prev)
    return pl.pallas_call(
        functools.partial(_norm2_kernel, n_real=n_real),
        grid=(n_real + n_zero,),
        in_specs=in_specs,
        out_specs=[pl.BlockSpec((tm, d), lambda i: (i + off, 0)),
                   pl.BlockSpec((tm, V7X_LANES), lambda i: (i + off, 0)),
                   pl.BlockSpec((tm, V7X_LANES), lambda i: (i + off, 0))],
        out_shape=[jax.ShapeDtypeStruct((n_total, d), F32),
                   jax.ShapeDtypeStruct((n_total, V7X_LANES), jnp.int32),
                   jax.ShapeDtypeStruct((n_total, V7X_LANES), F32)],
        input_output_aliases=aliases,
        compiler_params=_params(("parallel",), _vmem_limit(
            [_nbytes((tm, d), F32) * (2 if rows == 1 else 4)],
            temp_bytes=3 * _nbytes((tm, d), F32))),
        name="norm2_router",
    )(*args)


def _proj_kernel(a_ref, w_ref, *o_refs, scale, want_f32, want_bf16):
    acc = jnp.dot(a_ref[...], w_ref[...], preferred_element_type=F32)
    k = 0
    if want_f32:
        o_refs[k][...] = acc
        k += 1
    if want_bf16:
        o_refs[k][...] = (acc * scale).astype(BF16)


def _proj(a, w, col0, n, tm, tn, *, scale=1.0, want_f32, want_bf16):
    m, kd = a.shape
    cb0 = col0 // tn
    out_specs, out_shape, ob = [], [], []
    if want_f32:
        out_specs.append(pl.BlockSpec((tm, tn), lambda i, j: (i, j)))
        out_shape.append(jax.ShapeDtypeStruct((m, n), F32))
        ob.append(_nbytes((tm, tn), F32))
    if want_bf16:
        out_specs.append(pl.BlockSpec((tm, tn), lambda i, j: (i, j)))
        out_shape.append(jax.ShapeDtypeStruct((m, n), BF16))
        ob.append(_nbytes((tm, tn), BF16))
    return pl.pallas_call(
        functools.partial(_proj_kernel, scale=scale, want_f32=want_f32, want_bf16=want_bf16),
        grid=(m // tm, n // tn),
        in_specs=[pl.BlockSpec((tm, kd), lambda i, j: (i, 0)),
                  pl.BlockSpec((kd, tn), lambda i, j: (0, cb0 + j))],
        out_specs=out_specs,
        out_shape=out_shape,
        compiler_params=_params(("parallel", "arbitrary"), _vmem_limit(
            [_nbytes((tm, kd), BF16), _nbytes((kd, tn), BF16)] + ob,
            temp_bytes=_nbytes((tm, tn), F32))),
        name="proj",
    )(a, w)


def _merge_kernel(osb_ref, ofx_ref, h_ref, wsb_ref, wfx_ref, wg1_ref, wg2_ref, bg1_ref, bg2_ref, o_ref):
    h = h_ref[...]
    bs = jnp.dot(osb_ref[...], wsb_ref[...], preferred_element_type=F32)
    bx = jnp.dot(ofx_ref[...], wfx_ref[...], preferred_element_type=F32)
    g1 = jax.nn.sigmoid(jnp.dot(h, wg1_ref[...], preferred_element_type=F32) + bg1_ref[...])
    g2 = jax.nn.sigmoid(jnp.dot(h, wg2_ref[...], preferred_element_type=F32) + bg2_ref[...])
    o_ref[...] = (g1 * bs + g2 * bx).astype(BF16)


def _merge(o_sb, o_fx, h, w_bsb, w_bfx, w_gate, b_gate, tm, tn):
    m, d = h.shape
    ws = o_sb.shape[1]
    wx = o_fx.shape[1]
    nb = d // tn
    return pl.pallas_call(
        _merge_kernel,
        grid=(m // tm, nb),
        in_specs=[pl.BlockSpec((tm, ws), lambda i, j: (i, 0)),
                  pl.BlockSpec((tm, wx), lambda i, j: (i, 0)),
                  pl.BlockSpec((tm, d), lambda i, j: (i, 0)),
                  pl.BlockSpec((ws, tn), lambda i, j: (0, j)),
                  pl.BlockSpec((wx, tn), lambda i, j: (0, j)),
                  pl.BlockSpec((d, tn), lambda i, j: (0, j)),
                  pl.BlockSpec((d, tn), lambda i, j: (0, nb + j)),
                  pl.BlockSpec((1, tn), lambda i, j: (0, j)),
                  pl.BlockSpec((1, tn), lambda i, j: (0, nb + j))],
        out_specs=pl.BlockSpec((tm, tn), lambda i, j: (i, j)),
        out_shape=jax.ShapeDtypeStruct((m, d), BF16),
        compiler_params=_params(("parallel", "arbitrary"), _vmem_limit(
            [_nbytes((tm, ws + wx + d), BF16), _nbytes((ws + wx + 2 * d, tn), BF16), _nbytes((tm, tn), BF16)],
            temp_bytes=6 * _nbytes((tm, tn), F32))),
        name="merge",
    )(o_sb, o_fx, h, w_bsb, w_bfx, w_gate, w_gate, b_gate, b_gate)


def _outproj_kernel(a_ref, w_ref, x_ref, g_ref, o_ref):
    acc = jnp.dot(a_ref[...], w_ref[...], preferred_element_type=F32)
    o_ref[...] = x_ref[...] + g_ref[...] * acc


def _outproj(a, w, x, gate, tm, tn):
    m, kd = a.shape
    n = w.shape[1]
    rows = gate.shape[0]
    g_spec = pl.BlockSpec((1, tn), lambda i, j: (0, j)) if rows == 1 else pl.BlockSpec((tm, tn), lambda i, j: (i, j))
    return pl.pallas_call(
        _outproj_kernel,
        grid=(m // tm, n // tn),
        in_specs=[pl.BlockSpec((tm, kd), lambda i, j: (i, 0)),
                  pl.BlockSpec((kd, tn), lambda i, j: (0, j)),
                  pl.BlockSpec((tm, tn), lambda i, j: (i, j)),
                  g_spec],
        out_specs=pl.BlockSpec((tm, tn), lambda i, j: (i, j)),
        out_shape=jax.ShapeDtypeStruct((m, n), F32),
        compiler_params=_params(("parallel", "arbitrary"), _vmem_limit(
            [_nbytes((tm, kd), BF16), _nbytes((kd, tn), BF16), 3 * _nbytes((tm, tn), F32)],
            temp_bytes=_nbytes((tm, tn), F32))),
        name="outproj",
    )(a, w, x, gate)


def _cumsum_kernel(lf_ref, tri_ref, o_ref, *, chunk):
    t = lf_ref.shape[1]
    nh = lf_ref.shape[2]
    carry = jnp.zeros((1, nh), F32)
    start = 0
    while start < t:
        c = min(chunk, t - start)
        x = lf_ref[0, start:start + c, :]
        tri = tri_ref[:c, :c]
        hi, mid, lo = _split3(x)
        s = (jnp.dot(tri, hi, preferred_element_type=F32)
             + jnp.dot(tri, mid, preferred_element_type=F32)
             + jnp.dot(tri, lo, preferred_element_type=F32)) + carry
        o_ref[0, start:start + c, :] = s
        carry = s[c - 1:c, :]
        start += c


def _cumsum_time(lf, tri_incl, chunk):
    b, t, nh = lf.shape
    return pl.pallas_call(
        functools.partial(_cumsum_kernel, chunk=chunk),
        grid=(b,),
        in_specs=[pl.BlockSpec((1, t, nh), lambda i: (i, 0, 0)),
                  pl.BlockSpec((chunk, chunk), lambda i: (0, 0))],
        out_specs=pl.BlockSpec((1, t, nh), lambda i: (i, 0, 0)),
        out_shape=jax.ShapeDtypeStruct((b, t, nh), F32),
        compiler_params=_params(("parallel",), _vmem_limit(
            [2 * _nbytes((t, V7X_LANES), F32)], temp_bytes=4 << 20)),
        name="cumsum_logf",
    )(lf, tri_incl)


def _qk(q, k):
    if q.ndim == 3:
        return jnp.einsum("hqd,hkd->hqk", q, k, preferred_element_type=F32)
    return lax.dot_general(q, k, (((1,), (1,)), ((), ())), preferred_element_type=F32)


def _pv(p, v):
    if p.ndim == 3:
        return jnp.einsum("hqk,hkd->hqd", p, v, preferred_element_type=F32)
    return jnp.dot(p, v, preferred_element_type=F32)


def _rows_dot(x, m):
    if x.ndim == 3:
        h, r, n = x.shape
        return jnp.dot(x.reshape(h * r, n), m, preferred_element_type=F32).reshape(h, r, n)
    return jnp.dot(x, m, preferred_element_type=F32)


def _sb_step(q, k, v, carry, acc, tri, mask):
    hw = tri.shape[0]
    groups = k.shape[-2] // hw
    z = _qk(q, k)
    ls = -(jnp.maximum(z, 0.0) + jnp.log(1.0 + jnp.exp2(-jnp.abs(z))) * LOG2E)
    lsm = ls if mask is None else jnp.where(mask, ls, 0.0)
    hi = lsm.astype(BF16)
    lo = (lsm - hi.astype(F32)).astype(BF16)
    lw = z + ls
    parts = [None] * groups
    for g in reversed(range(groups)):
        c0 = g * hw
        cs = _rows_dot(hi[..., c0:c0 + hw], tri) + _rows_dot(lo[..., c0:c0 + hw], tri)
        parts[g] = jnp.exp2(lw[..., c0:c0 + hw] + cs + carry)
        carry = carry + cs[..., :1] + lsm[..., c0:c0 + 1]
    w = parts[0] if groups == 1 else jnp.concatenate(parts, axis=-1)
    if mask is not None:
        w = jnp.where(mask, w, 0.0)
    acc = acc + _pv(w.astype(BF16), v)
    return carry, acc


def _strict_lower_mask(tq, tk):
    row = lax.broadcasted_iota(jnp.int32, (tq, tk), 0)
    col = lax.broadcasted_iota(jnp.int32, (tq, tk), 1)
    return col < row


def _head_group(ref, rows, hp):
    return jnp.stack([ref[rows, h * HEAD_DIM:(h + 1) * HEAD_DIM] for h in range(hp)])


def _store_head_group(o_ref, x):
    for h in range(x.shape[0]):
        o_ref[:, h * HEAD_DIM:(h + 1) * HEAD_DIM] = x[h].astype(o_ref.dtype)


def _sb_prompt_kernel(q_ref, k_ref, v_ref, tri_ref, o_ref, *, tq, hp):
    i = pl.program_id(1)
    q = _head_group(q_ref, slice(None), hp)
    tri = tri_ref[...]
    d0 = pl.multiple_of(i * tq, tq)
    carry = jnp.zeros((hp, tq, 1), F32)
    acc = jnp.zeros((hp, tq, HEAD_DIM), F32)
    carry, acc = _sb_step(q, _head_group(k_ref, pl.ds(d0, tq), hp), _head_group(v_ref, pl.ds(d0, tq), hp),
                          carry, acc, tri, _strict_lower_mask(tq, tq))

    def body(n, c):
        r0 = pl.multiple_of((i - 1 - n) * tq, tq)
        return _sb_step(q, _head_group(k_ref, pl.ds(r0, tq), hp), _head_group(v_ref, pl.ds(r0, tq), hp),
                        c[0], c[1], tri, None)

    carry, acc = lax.fori_loop(0, i, body, (carry, acc))
    _store_head_group(o_ref, acc)


def _sb_prompt(q, k, v, tri, n_heads, tq, hp):
    t = q.shape[0]
    wb = hp * HEAD_DIM
    return pl.pallas_call(
        functools.partial(_sb_prompt_kernel, tq=tq, hp=hp),
        grid=(n_heads // hp, t // tq),
        in_specs=[pl.BlockSpec((tq, wb), lambda g, i: (i, g)),
                  pl.BlockSpec((t, wb), lambda g, i: (0, g)),
                  pl.BlockSpec((t, wb), lambda g, i: (0, g)),
                  pl.BlockSpec(tri.shape, lambda g, i: (0, 0))],
        out_specs=pl.BlockSpec((tq, wb), lambda g, i: (i, g)),
        out_shape=jax.ShapeDtypeStruct((t, n_heads * HEAD_DIM), BF16),
        compiler_params=_params(("parallel", "arbitrary"), _vmem_limit(
            [2 * _nbytes((t, wb), BF16), _nbytes(tri.shape, BF16)],
            temp_bytes=12 * hp * _nbytes((tq, tq), F32))),
        name="sb_prompt",
    )(q, k, v, tri)


def _past_heads(ref, n_heads, tk):
    return jnp.stack([ref[pl.ds(h, tk, stride=n_heads), :] for h in range(n_heads)]).astype(BF16)


def _new_heads(ref, n_heads):
    return jnp.stack([ref[0, :, h * HEAD_DIM:(h + 1) * HEAD_DIM] for h in range(n_heads)])


def _sb_sample_kernel(q_ref, kn_ref, vn_ref, kp_ref, vp_ref, tri_ref, trin_ref, o_ref, carry_s, acc_s,
                      *, n_heads, tk):
    j = pl.program_id(1)
    tq = q_ref.shape[1]
    q = _new_heads(q_ref, n_heads)

    @pl.when(j == 0)
    def _():
        c, a = _sb_step(q, _new_heads(kn_ref, n_heads), _new_heads(vn_ref, n_heads),
                        jnp.zeros((n_heads, tq, 1), F32), jnp.zeros((n_heads, tq, HEAD_DIM), F32),
                        trin_ref[...], _strict_lower_mask(tq, tq))
        carry_s[...] = jnp.broadcast_to(c, carry_s.shape)
        acc_s[...] = a

    c, a = _sb_step(q, _past_heads(kp_ref, n_heads, tk), _past_heads(vp_ref, n_heads, tk),
                    carry_s[:, :, :1], acc_s[...], tri_ref[...], None)
    carry_s[...] = jnp.broadcast_to(c, carry_s.shape)
    acc_s[...] = a

    @pl.when(j == pl.num_programs(1) - 1)
    def _():
        for h in range(n_heads):
            o_ref[0, :, h * HEAD_DIM:(h + 1) * HEAD_DIM] = acc_s[h].astype(BF16)


def _sb_sample(q, k_new, v_new, k_past, v_past, tri, tri_new, n_heads, tk):
    b, tq, width = q.shape
    p = k_past.shape[0] // (b * n_heads)
    nc = p // tk
    new_spec = pl.BlockSpec((1, tq, width), lambda bi, j: (bi, 0, 0))
    past_spec = pl.BlockSpec((tk * n_heads, HEAD_DIM), lambda bi, j: (bi * nc + (nc - 1 - j), 0))
    return pl.pallas_call(
        functools.partial(_sb_sample_kernel, n_heads=n_heads, tk=tk),
        grid=(b, nc),
        in_specs=[new_spec, new_spec, new_spec, past_spec, past_spec,
                  pl.BlockSpec(tri.shape, lambda bi, j: (0, 0)),
                  pl.BlockSpec(tri_new.shape, lambda bi, j: (0, 0))],
        out_specs=new_spec,
        out_shape=jax.ShapeDtypeStruct((b, tq, width), BF16),
        scratch_shapes=[pltpu.VMEM((n_heads, tq, V7X_LANES), F32), pltpu.VMEM((n_heads, tq, HEAD_DIM), F32)],
        compiler_params=_params(("parallel", "arbitrary"), _vmem_limit(
            [2 * _nbytes((tk * n_heads, HEAD_DIM), F32), 4 * _nbytes((tq, width), BF16)],
            scratch_bytes=2 * _nbytes((n_heads, tq, V7X_LANES), F32), temp_bytes=8 << 20)),
        name="sb_sample",
    )(q, k_new, v_new, k_past, v_past, tri, tri_new)


def _fox_step(q, k, v, fk_row, fq_col, m, l, acc, mask):
    s = _qk(q, k) - fk_row
    if mask is not None:
        s = jnp.where(mask, s, NEG_BIG)
    m_new = jnp.maximum(m, jnp.max(s, axis=-1, keepdims=True) + fq_col)
    pexp = jnp.exp2(s + (fq_col - m_new))
    alpha = jnp.exp2(m - m_new)
    l = alpha * l + jnp.sum(pexp, axis=-1, keepdims=True)
    acc = alpha * acc + _pv(pexp.astype(BF16), v)
    return m_new, l, acc


def _lower_incl_mask(tq, tk):
    row = lax.broadcasted_iota(jnp.int32, (tq, tk), 0)
    col = lax.broadcasted_iota(jnp.int32, (tq, tk), 1)
    return col <= row


def _pick_head_column(f_cols, h):
    lane = lax.broadcasted_iota(jnp.int32, f_cols.shape, 1)
    return jnp.sum(jnp.where(lane == h, f_cols, 0.0), axis=1, keepdims=True)


def _fox_prompt_kernel(q_ref, k_ref, v_ref, fr_ref, fc_ref, o_ref, *, tq):
    h = pl.program_id(0)
    i = pl.program_id(1)
    q = q_ref[...]
    fq = _pick_head_column(fc_ref[...], h) * LOG2E
    d0 = pl.multiple_of(i * tq, tq)
    m = jnp.full((tq, 1), NEG_BIG, F32)
    l = jnp.zeros((tq, 1), F32)
    acc = jnp.zeros((tq, HEAD_DIM), F32)
    m, l, acc = _fox_step(q, k_ref[pl.ds(d0, tq), :], v_ref[pl.ds(d0, tq), :],
                          fr_ref[0, :, pl.ds(d0, tq)] * LOG2E, fq, m, l, acc, _lower_incl_mask(tq, tq))

    def keys(r0, w, c):
        return _fox_step(q, k_ref[pl.ds(r0, w), :], v_ref[pl.ds(r0, w), :],
                         fr_ref[0, :, pl.ds(r0, w)] * LOG2E, fq, c[0], c[1], c[2], None)

    state = lax.fori_loop(0, i // 2, lambda n, c: keys(pl.multiple_of(n * 2 * tq, 2 * tq), 2 * tq, c), (m, l, acc))
    m, l, acc = lax.cond(i % 2 == 1, lambda c: keys(pl.multiple_of((i - 1) * tq, tq), tq, c), lambda c: c, state)
    o_ref[...] = (acc / l).astype(BF16)


def _fox_prompt(q, k, v, f_rows, f_cols, n_heads, tq):
    t = q.shape[0]
    return pl.pallas_call(
        functools.partial(_fox_prompt_kernel, tq=tq),
        grid=(n_heads, t // tq),
        in_specs=[pl.BlockSpec((tq, HEAD_DIM), lambda h, i: (i, h)),
                  pl.BlockSpec((t, HEAD_DIM), lambda h, i: (0, h)),
                  pl.BlockSpec((t, HEAD_DIM), lambda h, i: (0, h)),
                  pl.BlockSpec((1, 1, t), lambda h, i: (h, 0, 0)),
                  pl.BlockSpec((tq, n_heads), lambda h, i: (i, 0))],
        out_specs=pl.BlockSpec((tq, HEAD_DIM), lambda h, i: (i, h)),
        out_shape=jax.ShapeDtypeStruct((t, n_heads * HEAD_DIM), BF16),
        compiler_params=_params(("parallel", "arbitrary"), _vmem_limit(
            [2 * _nbytes((t, HEAD_DIM), BF16), _nbytes((8, t), F32)],
            temp_bytes=10 * _nbytes((tq, 2 * tq), F32))),
        name="fox_prompt",
    )(q, k, v, f_rows, f_cols)


def _fox_sample_kernel(q_ref, kn_ref, vn_ref, kp_ref, vp_ref, frp_ref, frn_ref, fc_ref, o_ref, m_s, l_s, acc_s,
                       *, n_heads, tk):
    j = pl.program_id(1)
    tq = q_ref.shape[1]
    q = _new_heads(q_ref, n_heads)
    fc = fc_ref[0] * LOG2E
    fq = jnp.stack([fc[:, h:h + 1] for h in range(n_heads)])

    @pl.when(j == 0)
    def _():
        m, l, a = _fox_step(q, _new_heads(kn_ref, n_heads), _new_heads(vn_ref, n_heads),
                            (frn_ref[0] * LOG2E)[:, None, :], fq,
                            jnp.full((n_heads, tq, 1), NEG_BIG, F32), jnp.zeros((n_heads, tq, 1), F32),
                            jnp.zeros((n_heads, tq, HEAD_DIM), F32), _lower_incl_mask(tq, tq))
        m_s[...] = jnp.broadcast_to(m, m_s.shape)
        l_s[...] = jnp.broadcast_to(l, l_s.shape)
        acc_s[...] = a

    m, l, a = _fox_step(q, _past_heads(kp_ref, n_heads, tk), _past_heads(vp_ref, n_heads, tk),
                        (frp_ref[0] * LOG2E)[:, None, :], fq,
                        m_s[:, :, :1], l_s[:, :, :1], acc_s[...], None)
    m_s[...] = jnp.broadcast_to(m, m_s.shape)
    l_s[...] = jnp.broadcast_to(l, l_s.shape)
    acc_s[...] = a

    @pl.when(j == pl.num_programs(1) - 1)
    def _():
        for h in range(n_heads):
            o_ref[0, :, h * HEAD_DIM:(h + 1) * HEAD_DIM] = (acc_s[h] / l_s[h][:, :1]).astype(BF16)


def _fox_sample(q, k_new, v_new, k_past, v_past, f_rows_past, f_rows_new, f_cols_new, n_heads, tk):
    b, tq, width = q.shape
    p = f_rows_past.shape[2]
    nc = p // tk
    new_spec = pl.BlockSpec((1, tq, width), lambda bi, j: (bi, 0, 0))
    past_spec = pl.BlockSpec((tk * n_heads, HEAD_DIM), lambda bi, j: (bi * nc + j, 0))
    return pl.pallas_call(
        functools.partial(_fox_sample_kernel, n_heads=n_heads, tk=tk),
        grid=(b, nc),
        in_specs=[new_spec, new_spec, new_spec, past_spec, past_spec,
                  pl.BlockSpec((1, n_heads, tk), lambda bi, j: (bi, 0, j)),
                  pl.BlockSpec((1, n_heads, tq), lambda bi, j: (bi, 0, 0)),
                  pl.BlockSpec((1, tq, n_heads), lambda bi, j: (bi, 0, 0))],
        out_specs=new_spec,
        out_shape=jax.ShapeDtypeStruct((b, tq, width), BF16),
        scratch_shapes=[pltpu.VMEM((n_heads, tq, V7X_LANES), F32), pltpu.VMEM((n_heads, tq, V7X_LANES), F32),
                        pltpu.VMEM((n_heads, tq, HEAD_DIM), F32)],
        compiler_params=_params(("parallel", "arbitrary"), _vmem_limit(
            [2 * _nbytes((tk * n_heads, HEAD_DIM), F32), 4 * _nbytes((tq, width), BF16)],
            scratch_bytes=3 * _nbytes((n_heads, tq, V7X_LANES), F32), temp_bytes=8 << 20)),
        name="fox_sample",
    )(q, k_new, v_new, k_past, v_past, f_rows_past, f_rows_new, f_cols_new)


def _row_copy(src_hbm, dst, row_src, row_dst, sem):
    return pltpu.make_async_copy(src_hbm.at[pl.ds(row_src, 1)], dst.at[pl.ds(row_dst, 1)], sem)


def _gather_kernel(nv_ref, cur_ref, nxt_ref, x_hbm, o_ref, buf, sem, *, tg):
    i = pl.program_id(0)
    nv = nv_ref[0]

    def issue(idx_ref, slot):
        def body(r, _):
            _row_copy(x_hbm, buf.at[slot], idx_ref[r], r, sem.at[slot]).start()
            return 0

        lax.fori_loop(0, tg, body, 0, unroll=8)

    @pl.when(i == 0)
    def _():
        issue(cur_ref, 0)

    @pl.when(i + 1 < nv)
    def _():
        issue(nxt_ref, (i + 1) % 2)

    @pl.when(i < nv)
    def _():
        slot = i % 2
        pltpu.make_async_copy(x_hbm.at[pl.ds(0, tg)], buf.at[slot], sem.at[slot]).wait()
        o_ref[...] = buf[slot].astype(BF16)

    @pl.when(i >= nv)
    def _():
        o_ref[...] = jnp.zeros(o_ref.shape, BF16)


def _gather_tokens(n_valid_tiles, src, x, n_rows, tg):
    d = x.shape[1]
    nt = n_rows // tg
    return pl.pallas_call(
        functools.partial(_gather_kernel, tg=tg),
        grid_spec=pltpu.PrefetchScalarGridSpec(
            num_scalar_prefetch=1,
            grid=(nt,),
            in_specs=[pl.BlockSpec((tg,), lambda i, nv: (jnp.minimum(i, nv[0] - 1),), memory_space=pltpu.SMEM),
                      pl.BlockSpec((tg,), lambda i, nv: (jnp.minimum(i + 1, nv[0] - 1),), memory_space=pltpu.SMEM),
                      pl.BlockSpec(memory_space=pl.ANY)],
            out_specs=pl.BlockSpec((tg, d), lambda i, nv: (i, 0)),
            scratch_shapes=[pltpu.VMEM((2, tg, d), F32), pltpu.SemaphoreType.DMA((2,))]),
        out_shape=jax.ShapeDtypeStruct((n_rows, d), BF16),
        compiler_params=_params(("arbitrary",), _vmem_limit(
            [_nbytes((tg, d), BF16)], scratch_bytes=_nbytes((2, tg, d), F32), temp_bytes=_nbytes((tg, d), F32))),
        name="gather_tokens",
    )(n_valid_tiles, src, src, x)


def _tile_meta(nv_ref, tv_ref):
    t = pl.program_id(0)
    return t < nv_ref[0], tv_ref[t]


def _moe_up_kernel(nv_ref, te_ref, tv_ref, x_ref, wg_ref, wu_ref, bg_ref, bu_ref, a_ref, wg_bf, wu_bf, *, sub):
    active, valid = _tile_meta(nv_ref, tv_ref)

    @pl.when(active)
    def _():
        wg_bf[...] = wg_ref[0].astype(BF16)
        wu_bf[...] = wu_ref[0].astype(BF16)

    tm = x_ref.shape[0]
    for s in range(tm // sub):
        rows = pl.ds(s * sub, sub)
        has_rows = (s + 1) * sub > tm - valid

        @pl.when(jnp.logical_and(active, has_rows))
        def _():
            x = x_ref[rows, :]
            gate = jnp.minimum(jnp.dot(x, wg_bf[...], preferred_element_type=F32) + bg_ref[0], SWIGLU_LIMIT)
            up = jnp.clip(jnp.dot(x, wu_bf[...], preferred_element_type=F32) + bu_ref[0],
                          -SWIGLU_LIMIT, SWIGLU_LIMIT)
            a_ref[rows, :] = (gate * jax.nn.sigmoid(SWIGLU_ALPHA * gate) * (up + 1.0)).astype(BF16)

        @pl.when(jnp.logical_not(jnp.logical_and(active, has_rows)))
        def _():
            a_ref[rows, :] = jnp.zeros((sub, a_ref.shape[1]), BF16)


def _clamped_tile(t, nv):
    return jnp.minimum(t, nv[0] - 1)


def _moe_up(n_valid_tiles, tile_expert, tile_valid, x_sorted, w_gate, b_gate, w_up, b_up, tm, tn, sub):
    r, d = x_sorted.shape
    n_exp, _, f = w_gate.shape
    w_spec = pl.BlockSpec((1, d, tn), lambda t, n, nv, te, tv: (te[_clamped_tile(t, nv)], 0, n))
    b_spec = pl.BlockSpec((1, 1, tn), lambda t, n, nv, te, tv: (te[_clamped_tile(t, nv)], 0, n))
    return pl.pallas_call(
        functools.partial(_moe_up_kernel, sub=sub),
        grid_spec=pltpu.PrefetchScalarGridSpec(
            num_scalar_prefetch=3,
            grid=(r // tm, f // tn),
            in_specs=[pl.BlockSpec((tm, d), lambda t, n, nv, te, tv: (_clamped_tile(t, nv), 0)),
                      w_spec, w_spec, b_spec, b_spec],
            out_specs=pl.BlockSpec((tm, tn), lambda t, n, nv, te, tv: (t, n)),
            scratch_shapes=[pltpu.VMEM((d, tn), BF16), pltpu.VMEM((d, tn), BF16)]),
        out_shape=jax.ShapeDtypeStruct((r, f), BF16),
        compiler_params=_params(("arbitrary", "arbitrary"), _vmem_limit(
            [_nbytes((tm, d), BF16), 2 * _nbytes((d, tn), F32), _nbytes((tm, tn), BF16)],
            scratch_bytes=2 * _nbytes((d, tn), BF16), temp_bytes=6 * _nbytes((sub, tn), F32))),
        name="moe_up",
    )(n_valid_tiles, tile_expert, tile_valid, x_sorted, w_gate, w_up,
      b_gate.reshape(n_exp, 1, f), b_up.reshape(n_exp, 1, f))


def _moe_down_kernel(nv_ref, te_ref, tv_ref, a_ref, wd_ref, bd_ref, cw_ref, y_ref, wd_bf, *, sub):
    active, valid = _tile_meta(nv_ref, tv_ref)

    @pl.when(active)
    def _():
        wd_bf[...] = wd_ref[0].astype(BF16)

    tm = a_ref.shape[0]
    for s in range(tm // sub):
        rows = pl.ds(s * sub, sub)
        has_rows = (s + 1) * sub > tm - valid

        @pl.when(jnp.logical_and(active, has_rows))
        def _():
            y = jnp.dot(a_ref[rows, :], wd_bf[...], preferred_element_type=F32) + bd_ref[0]
            y_ref[rows, :] = cw_ref[rows, :] * y

        @pl.when(jnp.logical_not(jnp.logical_and(active, has_rows)))
        def _():
            y_ref[rows, :] = jnp.zeros((sub, y_ref.shape[1]), F32)


def _moe_down(n_valid_tiles, tile_expert, tile_valid, act, w_down, b_down, cw_sorted, tm, tn, sub):
    r, f = act.shape
    n_exp, _, d = w_down.shape
    return pl.pallas_call(
        functools.partial(_moe_down_kernel, sub=sub),
        grid_spec=pltpu.PrefetchScalarGridSpec(
            num_scalar_prefetch=3,
            grid=(r // tm, d // tn),
            in_specs=[pl.BlockSpec((tm, f), lambda t, n, nv, te, tv: (_clamped_tile(t, nv), 0)),
                      pl.BlockSpec((1, f, tn), lambda t, n, nv, te, tv: (te[_clamped_tile(t, nv)], 0, n)),
                      pl.BlockSpec((1, 1, tn), lambda t, n, nv, te, tv: (te[_clamped_tile(t, nv)], 0, n)),
                      pl.BlockSpec((tm, 1), lambda t, n, nv, te, tv: (_clamped_tile(t, nv), 0))],
            out_specs=pl.BlockSpec((tm, tn), lambda t, n, nv, te, tv: (t, n)),
            scratch_shapes=[pltpu.VMEM((f, tn), BF16)]),
        out_shape=jax.ShapeDtypeStruct((r, d), F32),
        compiler_params=_params(("arbitrary", "arbitrary"), _vmem_limit(
            [_nbytes((tm, f), BF16), _nbytes((f, tn), F32), _nbytes((tm, tn), F32),
             _nbytes((tm, V7X_LANES), F32)],
            scratch_bytes=_nbytes((f, tn), BF16), temp_bytes=3 * _nbytes((sub, tn), F32))),
        name="moe_down",
    )(n_valid_tiles, tile_expert, tile_valid, act, w_down, b_down.reshape(n_exp, 1, d), cw_sorted)


def _combine_kernel(cur_ref, nxt_ref, y_hbm, xp_ref, xs_ref, gp_ref, gs_ref, lnf_ref, op_ref, os_ref, buf, sem,
                    *, n_tiles_p, tc):
    i = pl.program_id(0)
    n = pl.num_programs(0)

    def issue(idx_ref, slot):
        def body(r, _):
            for k in range(TOP_K):
                _row_copy(y_hbm, buf.at[slot, k], idx_ref[r * TOP_K + k], r, sem.at[slot]).start()
            return 0

        lax.fori_loop(0, tc, body, 0, unroll=4)

    @pl.when(i == 0)
    def _():
        issue(cur_ref, 0)

    @pl.when(i + 1 < n)
    def _():
        issue(nxt_ref, (i + 1) % 2)

    slot = i % 2
    for k in range(TOP_K):
        pltpu.make_async_copy(y_hbm.at[pl.ds(0, tc)], buf.at[slot, k], sem.at[slot]).wait()
    ffn = (buf[slot, 0] + buf[slot, 1]) + (buf[slot, 2] + buf[slot, 3])

    def finish(x, g):
        x2 = x + g * ffn
        ms = jnp.mean(x2 * x2, axis=-1, keepdims=True)
        return x2 * lax.rsqrt(ms + EPS) * lnf_ref[...]

    @pl.when(i < n_tiles_p)
    def _():
        op_ref[...] = finish(xp_ref[...], gp_ref[...])

    @pl.when(i >= n_tiles_p)
    def _():
        os_ref[...] = finish(xs_ref[...], gs_ref[...])


def _combine(pos_flat, y_sorted, x1_p, x1_s, gm_p, gm_s, ln_final_g, tc):
    n_p, d = x1_p.shape
    n_s = x1_s.shape[0]
    ntp = n_p // tc
    nts = n_s // tc
    nt = ntp + nts
    p_map = lambda i: (jnp.minimum(i, ntp - 1), 0)
    s_map = lambda i: (jnp.maximum(i - ntp, 0), 0)
    tp = tc * TOP_K
    return pl.pallas_call(
        functools.partial(_combine_kernel, n_tiles_p=ntp, tc=tc),
        grid=(nt,),
        in_specs=[pl.BlockSpec((tp,), lambda i: (i,), memory_space=pltpu.SMEM),
                  pl.BlockSpec((tp,), lambda i: (jnp.minimum(i + 1, nt - 1),), memory_space=pltpu.SMEM),
                  pl.BlockSpec(memory_space=pl.ANY),
                  pl.BlockSpec((tc, d), p_map),
                  pl.BlockSpec((tc, d), s_map),
                  pl.BlockSpec((1, d), lambda i: (0, 0)),
                  pl.BlockSpec((tc, d), s_map),
                  pl.BlockSpec((1, d), lambda i: (0, 0))],
        out_specs=[pl.BlockSpec((tc, d), p_map), pl.BlockSpec((tc, d), s_map)],
        scratch_shapes=[pltpu.VMEM((2, TOP_K, tc, d), F32), pltpu.SemaphoreType.DMA((2,))],
        out_shape=[jax.ShapeDtypeStruct((n_p, d), F32), jax.ShapeDtypeStruct((n_s, d), F32)],
        compiler_params=_params(("arbitrary",), _vmem_limit(
            [5 * _nbytes((tc, d), F32)], scratch_bytes=_nbytes((2, TOP_K, tc, d), F32),
            temp_bytes=3 * _nbytes((tc, d), F32))),
        name="combine_final",
    )(pos_flat, pos_flat, y_sorted, x1_p, x1_s, gm_p, gm_s, ln_final_g)


def _route_metadata(top_idx, top_w, n_exp, tm):
    n_tok, k = top_idx.shape
    n_pairs = n_tok * k
    e_flat = top_idx.reshape(n_pairs)
    onehot = (e_flat[:, None] == jnp.arange(n_exp, dtype=jnp.int32)[None, :]).astype(jnp.int32)
    counts = jnp.sum(onehot, axis=0)
    rank = jnp.sum((jnp.cumsum(onehot, axis=0) - 1) * onehot, axis=1)
    padded = ((counts + tm - 1) // tm) * tm
    ends = jnp.cumsum(padded)
    first_row = ends - counts
    pos = (first_row[e_flat] + rank).astype(jnp.int32)
    n_tiles = (n_pairs + n_exp * (tm - 1)) // tm
    n_rows = n_tiles * tm
    tok = (jnp.arange(n_pairs, dtype=jnp.int32) // k)
    pair = jnp.stack([tok, lax.bitcast_convert_type(top_w.reshape(n_pairs), jnp.int32)], axis=1)
    placed = jnp.zeros((n_rows, 2), jnp.int32).at[pos].set(pair)
    src = placed[:, 0]
    cw_sorted = lax.bitcast_convert_type(placed[:, 1:2], F32)
    tile_start = jnp.arange(n_tiles, dtype=jnp.int32) * tm
    tile_expert = jnp.minimum(jnp.searchsorted(ends, tile_start, side="right"), n_exp - 1).astype(jnp.int32)
    tile_valid = jnp.clip(tile_start + tm - first_row[tile_expert], 0, tm).astype(jnp.int32)
    n_valid_tiles = (ends[n_exp - 1] // tm).astype(jnp.int32).reshape(1)
    return pos, src, cw_sorted, tile_expert, tile_valid, n_valid_tiles, n_rows


def _tri_strict(n):
    j = jnp.arange(n)[:, None]
    s = jnp.arange(n)[None, :]
    return (j > s).astype(BF16)


def _tri_incl(n):
    t = jnp.arange(n)[:, None]
    s = jnp.arange(n)[None, :]
    return (s <= t).astype(BF16)


def kernel(x_prompt, x_sample, cache_sb_k, cache_sb_v, cache_fox_k, cache_fox_v, cache_fox_logf, c_prompt, c_sample, ln_attn_g, ln_moe_g, w_ada, b_ada, w_in, b_forget, w_branch_sb, w_branch_fox, w_merge_gate, b_merge_gate, w_out, w_router, b_router, w_exp_gate, b_exp_gate, w_exp_up, b_exp_up, w_exp_down, b_exp_down, ln_final_g):
    bp, seq, d = x_prompt.shape
    bs, dseq, _ = x_sample.shape
    depth = w_in.shape[0]
    assert depth == 1 and bp == 1
    n_heads = b_forget.shape[1]
    width = n_heads * HEAD_DIM
    n_exp = w_router.shape[2]
    past = cache_sb_k.shape[2]
    n_p = bp * seq
    n_s = bs * dseq
    n_tok = n_p + n_s
    q_scale = HEAD_DIM ** -0.5 * LOG2E

    c_all = jnp.concatenate([c_prompt, c_sample], axis=0)
    pad = (-c_all.shape[0]) % 8
    c_all = jnp.pad(c_all, ((0, pad), (0, 0)))
    mod = _ada_mod(c_all, w_ada[0], b_ada[0])
    mod_p = jnp.split(mod[0:1], N_MOD, axis=-1)
    mod_s = [jnp.repeat(m_, dseq, axis=0) for m_ in jnp.split(mod[1:1 + bs], N_MOD, axis=-1)]

    w_in_b = w_in[0].astype(BF16)
    wf_pad = jnp.pad(w_in_b[:, 6 * width:6 * width + n_heads], ((0, 0), (0, V7X_LANES - n_heads)))
    w_bsb = w_branch_sb[0].astype(BF16)
    w_bfx = w_branch_fox[0].astype(BF16)
    w_gate = w_merge_gate[0].astype(BF16)
    b_gate = b_merge_gate[0].reshape(1, 2 * d)
    w_o = w_out[0].astype(BF16)
    wr = jnp.pad(w_router[0], ((0, 0), (0, V7X_LANES - n_exp)))
    wr_hi = wr.astype(BF16)
    wr_lo = (wr - wr_hi.astype(F32)).astype(BF16)
    br_pad = jnp.pad(b_router[0].reshape(1, n_exp), ((0, 0), (0, V7X_LANES - n_exp)), constant_values=-jnp.inf)
    g_attn = ln_attn_g[0].reshape(1, d)
    g_moe = ln_moe_g[0].reshape(1, d)
    bf = b_forget[0].reshape(1, n_heads)

    tq = 2 * V7X_MXU_DIM
    tk_s = 2 * V7X_MXU_DIM
    hp = 2
    tri_q = _tri_strict(V7X_MXU_DIM)
    tri_n = _tri_strict(dseq)
    tri_c = _tri_incl(V7X_MXU_DIM)

    def stream(x2d, mods, tm_norm, tm_mm, tn_mm):
        sh_a, sc_a, g_a = mods[0], mods[1], mods[2]
        h, lf = _norm1(x2d, g_attn, sc_a, sh_a, wf_pad, bf, tm_norm)
        pj = functools.partial(_proj, h, w_in_b, tm=tm_mm, tn=tn_mm)
        q_sb = pj(0 * width, width, scale=q_scale, want_f32=False, want_bf16=True)[0]
        k_sb, k_sb_b = pj(1 * width, width, want_f32=True, want_bf16=True)
        v_sb, v_sb_b = pj(2 * width, width, want_f32=True, want_bf16=True)
        q_fx = pj(3 * width, width, scale=q_scale, want_f32=False, want_bf16=True)[0]
        k_fx, k_fx_b = pj(4 * width, width, want_f32=True, want_bf16=True)
        v_fx, v_fx_b = pj(5 * width, width, want_f32=True, want_bf16=True)
        return h, lf, g_a, (q_sb, k_sb, v_sb, k_sb_b, v_sb_b), (q_fx, k_fx, v_fx, k_fx_b, v_fx_b)

    xp2 = x_prompt.reshape(n_p, d)
    h_p, lf_p, ga_p, sb_p, fx_p = stream(xp2, mod_p, 256, 1024, 512)
    o_sb_p = _sb_prompt(sb_p[0], sb_p[3], sb_p[4], tri_q, n_heads, tq, hp)
    fcum_p = _cumsum_time(lf_p.reshape(1, n_p, n_heads), tri_c, V7X_MXU_DIM)[0]
    f_rows_p = fcum_p.T.reshape(n_heads, 1, n_p)
    o_fx_p = _fox_prompt(fx_p[0], fx_p[3], fx_p[4], f_rows_p, fcum_p, n_heads, tq)
    merged_p = _merge(o_sb_p, o_fx_p, h_p, w_bsb, w_bfx, w_gate, b_gate, 1024, 256)
    x1_p = _outproj(merged_p, w_o, xp2, ga_p, 1024, 512)

    xs2 = x_sample.reshape(n_s, d)
    h_s, lf_s, ga_s, sb_s, fx_s = stream(xs2, mod_s, n_s, n_s, 512)
    r3 = lambda a: a.reshape(bs, dseq, width)
    c2 = lambda a: a.reshape(bs * past * n_heads, HEAD_DIM)
    o_sb_s = _sb_sample(r3(sb_s[0]), r3(sb_s[3]), r3(sb_s[4]), c2(cache_sb_k), c2(cache_sb_v),
                        tri_q, tri_n, n_heads, tk_s)
    lf_all = jnp.concatenate([cache_fox_logf[0].astype(F32), lf_s.reshape(bs, dseq, n_heads)], axis=1)
    fcum_s = _cumsum_time(lf_all, tri_c, V7X_MXU_DIM)
    f_rows_s = jnp.swapaxes(fcum_s, 1, 2)
    o_fx_s = _fox_sample(r3(fx_s[0]), r3(fx_s[3]), r3(fx_s[4]), c2(cache_fox_k), c2(cache_fox_v),
                         f_rows_s[:, :, :past], f_rows_s[:, :, past:], fcum_s[:, past:, :], n_heads, tk_s)
    merged_s = _merge(o_sb_s.reshape(n_s, width), o_fx_s.reshape(n_s, width), h_s,
                      w_bsb, w_bfx, w_gate, b_gate, n_s, 256)
    x1_s = _outproj(merged_s, w_o, xs2, ga_s, n_s, 512)

    routed = _norm2_router(x1_p, g_moe, mod_p[4], mod_p[3], wr_hi, wr_lo, br_pad, 256, n_tok, 0)
    h2, idx, tw = _norm2_router(x1_s, g_moe, mod_s[4], mod_s[3], wr_hi, wr_lo, br_pad, n_s, n_tok, n_p, prev=routed)
    sub = V7X_LANES
    tm_moe = -(-(n_tok * TOP_K * 9 // (8 * n_exp)) // (2 * sub)) * (2 * sub)
    pos, src, cw_sorted, tile_expert, tile_valid, n_valid_tiles, n_rows = _route_metadata(
        idx[:, :TOP_K], tw[:, :TOP_K], n_exp, tm_moe)
    tg = 256
    x_sorted = _gather_tokens(n_valid_tiles * (tm_moe // tg), src, h2, n_rows, tg)
    act = _moe_up(n_valid_tiles, tile_expert, tile_valid, x_sorted, w_exp_gate[0], b_exp_gate[0],
                  w_exp_up[0], b_exp_up[0], tm_moe, 256, sub)
    y_sorted = _moe_down(n_valid_tiles, tile_expert, tile_valid, act, w_exp_down[0], b_exp_down[0], cw_sorted,
                         tm_moe, 512, sub)
    y_p, y_s = _combine(pos, y_sorted, x1_p, x1_s, mod_p[5], mod_s[5], ln_final_g.reshape(1, d), 128)

    def rows5(a, b_, t_):
        return a.reshape(depth, b_, t_, n_heads, HEAD_DIM)

    return (y_p.reshape(bp, seq, d), y_s.reshape(bs, dseq, d),
            rows5(sb_p[1], bp, seq), rows5(sb_p[2], bp, seq), rows5(fx_p[1], bp, seq), rows5(fx_p[2], bp, seq),
            lf_p.reshape(depth, bp, seq, n_heads),
            rows5(sb_s[1], bs, dseq), rows5(sb_s[2], bs, dseq), rows5(fx_s[1], bs, dseq), rows5(fx_s[2], bs, dseq),
            lf_s.reshape(depth, bs, dseq, n_heads))
```
